```python
import math
import jax
import jax.numpy as jnp
from jax import lax
import numpy as np


D_MODEL = 1024
BATCH = 16
SEQ = 2048
DEPTH = 1
DEC_BATCH = 2
DEC_SEQ = 8192
PAST_LEN = 128

POOL_WINDOWS = (2, 4, 8, 16)
POOL_GROUPS = len(POOL_WINDOWS)
POOL_WIDTH = D_MODEL // 2
POOL_GROUP_DIM = POOL_WIDTH // POOL_GROUPS
N_HEADS = 8
QK_NOPE_DIM = 64
QK_ROPE_DIM = 32
QK_HEAD_DIM = QK_NOPE_DIM + QK_ROPE_DIM
V_HEAD_DIM = 64
Q_LORA_RANK = 256
KV_LORA_RANK = 128
ATTN_WIDTH = N_HEADS * V_HEAD_DIM
MIX_WIDTH = POOL_WIDTH + ATTN_WIDTH
IN_PROJ_WIDTH = POOL_WIDTH + Q_LORA_RANK + KV_LORA_RANK + QK_ROPE_DIM
Q_BLOCK = 128
ROPE_THETA = 10000.0
D_FF = 2816
CONV_WIDTH = 3
EPS = 1e-6

kernel_name = 'hymba_pool_mla_convffn_encoder'


def rms_norm(x, g):
    xf = x.astype(jnp.float32)
    y = xf * lax.rsqrt(jnp.mean(xf * xf, axis=-1, keepdims=True) + EPS)
    return (y * g.astype(jnp.float32)).astype(x.dtype)


def rope_tables(seq):
    inv_freq = ROPE_THETA ** (-jnp.arange(0, QK_ROPE_DIM, 2, dtype=jnp.float32) / QK_ROPE_DIM)
    ang = jnp.arange(seq, dtype=jnp.float32)[:, None] * inv_freq[None, :]
    return jnp.cos(ang), jnp.sin(ang)


def apply_rope(x, cos, sin):
    xf = x.astype(jnp.float32)
    x1, x2 = jnp.split(xf, 2, axis=-1)
    c = cos[:, None, :]
    s = sin[:, None, :]
    return jnp.concatenate([x1 * c - x2 * s, x2 * c + x1 * s], axis=-1).astype(x.dtype)


def pool_mixer(u, w_pool, pool_scale):
    B, S, _ = u.shape
    uf = u.astype(jnp.float32)
    cs = jnp.concatenate([jnp.zeros((B, 1, POOL_WIDTH), jnp.float32), jnp.cumsum(uf, axis=1)], axis=1)
    t = jnp.arange(S)
    outs = []
    for g, w in enumerate(POOL_WINDOWS):
        lo_c = g * POOL_GROUP_DIM
        hi_c = (g + 1) * POOL_GROUP_DIM
        lo = jnp.clip(t - w // 2, 0, S)
        hi = jnp.clip(t + w // 2, 0, S)
        csg = cs[:, :, lo_c:hi_c]
        window_sum = jnp.take(csg, hi, axis=1) - jnp.take(csg, lo, axis=1)
        count = (hi - lo).astype(jnp.float32)[None, :, None]
        outs.append(window_sum / count - uf[:, :, lo_c:hi_c])
    pooled = jnp.stack(outs, axis=2).astype(u.dtype)
    mixed = jnp.einsum('bsgc,gcd->bsgd', pooled, w_pool).reshape(B, S, POOL_WIDTH)
    return mixed * pool_scale


def mla(q_lat, kv_lat, k_rope_raw, q_norm_g, w_uq, kv_norm_g, w_ukv, cos, sin):
    B, S, _ = q_lat.shape
    q = jnp.einsum('bsr,rn->bsn', rms_norm(q_lat, q_norm_g), w_uq).reshape(B, S, N_HEADS, QK_HEAD_DIM)
    q_nope, q_rope = jnp.split(q, [QK_NOPE_DIM], axis=-1)
    q_rope = apply_rope(q_rope, cos, sin)
    kv = jnp.einsum('bsr,rn->bsn', rms_norm(kv_lat, kv_norm_g), w_ukv).reshape(B, S, N_HEADS, QK_NOPE_DIM + V_HEAD_DIM)
    k_nope, v = jnp.split(kv, [QK_NOPE_DIM], axis=-1)
    k_rope = apply_rope(k_rope_raw[:, :, None, :], cos, sin)
    q_full = jnp.concatenate([q_nope, q_rope], axis=-1)
    k_full = jnp.concatenate([k_nope, jnp.broadcast_to(k_rope, (B, S, N_HEADS, QK_ROPE_DIM))], axis=-1)
    scale = 1.0 / math.sqrt(QK_HEAD_DIM)
    n_blk = S // Q_BLOCK
    qb = q_full.reshape(B, n_blk, Q_BLOCK, N_HEADS, QK_HEAD_DIM).transpose(1, 0, 2, 3, 4)

    def attend(q_blk):
        s = jnp.einsum('bqhd,bkhd->bhqk', q_blk, k_full).astype(jnp.float32) * scale
        p = jax.nn.softmax(s, axis=-1).astype(v.dtype)
        return jnp.einsum('bhqk,bkhd->bqhd', p, v)

    o = lax.map(attend, qb)
    return o.transpose(1, 0, 2, 3, 4).reshape(B, S, ATTN_WIDTH)


def conv_ffn(h, w_up, conv_w, conv_b, w_down):
    S = h.shape[1]
    u = jnp.einsum('bsd,df->bsf', h, w_up)
    pad = CONV_WIDTH // 2
    up = jnp.pad(u, ((0, 0), (pad, pad), (0, 0)))
    c = conv_b
    for k in range(CONV_WIDTH):
        c = c + up[:, k:k + S] * conv_w[k]
    gate, val = jnp.split(c, 2, axis=-1)
    return jnp.einsum('bsf,fd->bsd', jax.nn.silu(gate) * val, w_down)


def encoder_forward(x, norm_mix_g, w_in, q_norm_g, w_uq, kv_norm_g, w_ukv, w_pool, pool_scale,
                    w_out, norm_ffn_g, w_up, conv_w, conv_b, w_down, final_norm_g):
    S = x.shape[1]
    cos, sin = rope_tables(S)
    for l in range(DEPTH):
        h = rms_norm(x, norm_mix_g[l])
        z = jnp.einsum('bsd,dn->bsn', h, w_in[l])
        u_pool, q_lat, kv_lat, k_rope_raw = jnp.split(
            z, [POOL_WIDTH, POOL_WIDTH + Q_LORA_RANK, POOL_WIDTH + Q_LORA_RANK + KV_LORA_RANK], axis=-1)
        y_pool = pool_mixer(u_pool, w_pool[l], pool_scale[l])
        y_attn = mla(q_lat, kv_lat, k_rope_raw, q_norm_g[l], w_uq[l], kv_norm_g[l], w_ukv[l], cos, sin)
        mixed = jnp.concatenate([y_pool, y_attn], axis=-1)
        x = x + jnp.einsum('bsm,md->bsd', mixed, w_out[l])
        h = rms_norm(x, norm_ffn_g[l])
        x = x + conv_ffn(h, w_up[l], conv_w[l], conv_b[l], w_down[l])
    return rms_norm(x, final_norm_g)


def setup_inputs(seed: int = 0) -> dict:
    key = jax.random.key(seed)
    ks = jax.random.split(key, 20)
    f32 = jnp.float32

    def normal(k, shape, scale):
        return jax.random.normal(k, shape, f32) * scale

    def gain(k, shape):
        return 1.0 + 0.02 * jax.random.normal(k, shape, f32)

    return {
        'x_prompt': normal(ks[0], (BATCH, SEQ, D_MODEL), 1.0),
        'x_sample': normal(ks[1], (DEC_BATCH, DEC_SEQ, D_MODEL), 1.0),
        'norm_mix_g': gain(ks[2], (DEPTH, D_MODEL)),
        'w_in': normal(ks[3], (DEPTH, D_MODEL, IN_PROJ_WIDTH), D_MODEL ** -0.5),
        'q_norm_g': gain(ks[4], (DEPTH, Q_LORA_RANK)),
        'w_uq': normal(ks[5], (DEPTH, Q_LORA_RANK, N_HEADS * QK_HEAD_DIM), Q_LORA_RANK ** -0.5),
        'kv_norm_g': gain(ks[6], (DEPTH, KV_LORA_RANK)),
        'w_ukv': normal(ks[7], (DEPTH, KV_LORA_RANK, N_HEADS * (QK_NOPE_DIM + V_HEAD_DIM)), KV_LORA_RANK ** -0.5),
        'w_pool': normal(ks[8], (DEPTH, POOL_GROUPS, POOL_GROUP_DIM, POOL_GROUP_DIM), POOL_GROUP_DIM ** -0.5),
        'pool_scale': gain(ks[9], (DEPTH, POOL_WIDTH)),
        'w_out': normal(ks[10], (DEPTH, MIX_WIDTH, D_MODEL), MIX_WIDTH ** -0.5),
        'norm_ffn_g': gain(ks[11], (DEPTH, D_MODEL)),
        'w_up': normal(ks[12], (DEPTH, D_MODEL, 2 * D_FF), D_MODEL ** -0.5),
        'conv_w': normal(ks[13], (DEPTH, CONV_WIDTH, 2 * D_FF), CONV_WIDTH ** -0.5),
        'conv_b': normal(ks[14], (DEPTH, 2 * D_FF), 0.02),
        'w_down': normal(ks[15], (DEPTH, D_FF, D_MODEL), D_FF ** -0.5),
        'final_norm_g': gain(ks[16], (D_MODEL,)),
    }


def reference(x_prompt, x_sample, norm_mix_g, w_in, q_norm_g, w_uq, kv_norm_g, w_ukv, w_pool, pool_scale,
              w_out, norm_ffn_g, w_up, conv_w, conv_b, w_down, final_norm_g):
    y_prompt = encoder_forward(x_prompt, norm_mix_g, w_in, q_norm_g, w_uq, kv_norm_g, w_ukv, w_pool, pool_scale,
                               w_out, norm_ffn_g, w_up, conv_w, conv_b, w_down, final_norm_g)
    y_sample = encoder_forward(x_sample, norm_mix_g, w_in, q_norm_g, w_uq, kv_norm_g, w_ukv, w_pool, pool_scale,
                               w_out, norm_ffn_g, w_up, conv_w, conv_b, w_down, final_norm_g)
    return (y_prompt, y_sample)
```

```python
import functools
import math

import jax
import jax.numpy as jnp
from jax import lax
from jax.experimental import pallas as pl
from jax.experimental.pallas import tpu as pltpu

D_MODEL = 1024
POOL_WINDOWS = (2, 4, 8, 16)
POOL_WIDTH = 512
POOL_GROUP_DIM = 128
N_HEADS = 8
QK_NOPE_DIM = 64
QK_ROPE_DIM = 32
QK_HEAD_DIM = 96
V_HEAD_DIM = 64
Q_LORA_RANK = 256
KV_LORA_RANK = 128
ROPE_THETA = 10000.0
D_FF = 2816
EPS = 1e-6

LANES = 128
HEAD_PAD = LANES
HALF_ROPE = QK_ROPE_DIM // 2
F_CHUNK = 256
N_F_CHUNKS = D_FF // F_CHUNK
HALO_F32 = 8
HALO_BF16 = 16
VMEM_LIMIT = 48 * 1024 * 1024

BF16 = jnp.bfloat16
F32 = jnp.float32


def _rms(x, g):
    ms = jnp.mean(x * x, axis=-1, keepdims=True)
    return x * lax.rsqrt(ms + EPS) * g


def _dot(a, b):
    return jnp.dot(a, b, preferred_element_type=F32)


def _pre_kernel(x_ref, gmix_ref, win_ref, gq_ref, gkv_ref, wqa_ref, wqb_ref,
                wk_ref, wv_ref, vones_ref, cos_ref, sin_ref,
                u_ref, q_ref, k_ref, v_ref, *, q_scale):
    x = x_ref[0]
    h = _rms(x, gmix_ref[...]).astype(BF16)
    z = _dot(h, win_ref[...])
    u_ref[0] = z[:, :POOL_WIDTH]
    o = POOL_WIDTH
    qn = _rms(z[:, o:o + Q_LORA_RANK], gq_ref[...]).astype(BF16)
    o += Q_LORA_RANK
    kvn = _rms(z[:, o:o + KV_LORA_RANK], gkv_ref[...]).astype(BF16)
    o += KV_LORA_RANK
    cos = cos_ref[...]
    sin = sin_ref[...]
    kr = z[:, o:o + LANES] * cos + z[:, o + LANES:o + 2 * LANES] * sin
    qa = _dot(qn, wqa_ref[...])
    qb = _dot(qn, wqb_ref[...])
    kk = _dot(kvn, wk_ref[...])
    vv = _dot(kvn, wv_ref[...]) + vones_ref[...]
    for hd in range(N_HEADS):
        sl = slice(hd * HEAD_PAD, (hd + 1) * HEAD_PAD)
        q_ref[0, hd] = ((qa[:, sl] * cos + qb[:, sl] * sin) * q_scale).astype(BF16)
        k_ref[0, hd] = (kk[:, sl] + kr).astype(BF16)
        v_ref[0, hd] = vv[:, sl].astype(BF16)


def _pre_call(x, p, cosf, sinf, tile):
    B, S, _ = x.shape
    grid = (B, S // tile)
    const2 = lambda b, i: (0, 0)
    q_scale = math.log2(math.e) / math.sqrt(QK_HEAD_DIM)
    head_shape = jax.ShapeDtypeStruct((B, N_HEADS, S, HEAD_PAD), BF16)
    head_spec = pl.BlockSpec((1, N_HEADS, tile, HEAD_PAD), lambda b, i: (b, 0, i, 0))
    return pl.pallas_call(
        functools.partial(_pre_kernel, q_scale=q_scale),
        grid=grid,
        in_specs=[
            pl.BlockSpec((1, tile, D_MODEL), lambda b, i: (b, i, 0)),
            pl.BlockSpec((1, D_MODEL), const2),
            pl.BlockSpec(p['w_in'].shape, const2),
            pl.BlockSpec((1, Q_LORA_RANK), const2),
            pl.BlockSpec((1, KV_LORA_RANK), const2),
            pl.BlockSpec(p['w_qa'].shape, const2),
            pl.BlockSpec(p['w_qb'].shape, const2),
            pl.BlockSpec(p['w_k'].shape, const2),
            pl.BlockSpec(p['w_v'].shape, const2),
            pl.BlockSpec(p['vones'].shape, const2),
            pl.BlockSpec((tile, LANES), lambda b, i: (i, 0)),
            pl.BlockSpec((tile, LANES), lambda b, i: (i, 0)),
        ],
        out_specs=[
            pl.BlockSpec((1, tile, POOL_WIDTH), lambda b, i: (b, i, 0)),
            head_spec, head_spec, head_spec,
        ],
        out_shape=[
            jax.ShapeDtypeStruct((B, S, POOL_WIDTH), F32),
            head_shape, head_shape, head_shape,
        ],
        compiler_params=pltpu.CompilerParams(
            dimension_semantics=("parallel", "parallel"),
            vmem_limit_bytes=VMEM_LIMIT),
        name="pre_proj",
    )(x, p['g_mix'], p['w_in'], p['g_q'], p['g_kv'], p['w_qa'], p['w_qb'],
      p['w_k'], p['w_v'], p['vones'], cosf, sinf)


def _attn_kernel(q_ref, k_ref, v_ref, o_ref, *, q_sub, k_chunk):
    tq = q_ref.shape[2]
    S = k_ref.shape[2]
    n_chunks = S // k_chunk
    lane = lax.broadcasted_iota(jnp.int32, (q_sub, LANES), 1)

    def attend(hd, row0):
        q = q_ref[0, hd, pl.ds(row0, q_sub), :]
        m = jnp.full((q_sub, 1), -1e30, F32)
        acc = jnp.zeros((q_sub, LANES), F32)
        for c in range(n_chunks):
            k = k_ref[0, hd, pl.ds(c * k_chunk, k_chunk), :]
            v = v_ref[0, hd, pl.ds(c * k_chunk, k_chunk), :]
            s = lax.dot_general(q, k, (((1,), (1,)), ((), ())),
                                preferred_element_type=F32)
            m_new = jnp.maximum(m, jnp.max(s, axis=-1, keepdims=True))
            pr = jnp.exp2(s - m_new).astype(BF16)
            pv = _dot(pr, v)
            if c == 0:
                acc = pv
            else:
                acc = acc * jnp.exp2(m - m_new) + pv
            m = m_new
        return acc

    def body(r, carry):
        row0 = pl.multiple_of(r * q_sub, q_sub)
        a0 = attend(0, row0)
        a1 = attend(1, row0)
        l0 = a0[:, V_HEAD_DIM:V_HEAD_DIM + 1]
        l1 = a1[:, 0:1]
        out = jnp.where(lane < V_HEAD_DIM, a0 / l0, a1 / l1)
        o_ref[0, pl.ds(row0, q_sub), :] = out.astype(o_ref.dtype)
        return carry

    lax.fori_loop(0, tq // q_sub, body, 0)


def _attn_call(q, k, v, tq, q_sub, k_chunk):
    B, H, S, _ = q.shape
    grid = (B, H // 2, S // tq)
    return pl.pallas_call(
        functools.partial(_attn_kernel, q_sub=q_sub, k_chunk=k_chunk),
        grid=grid,
        in_specs=[
            pl.BlockSpec((1, 2, tq, HEAD_PAD), lambda b, j, i: (b, j, i, 0)),
            pl.BlockSpec((1, 2, S, HEAD_PAD), lambda b, j, i: (b, j, 0, 0)),
            pl.BlockSpec((1, 2, S, HEAD_PAD), lambda b, j, i: (b, j, 0, 0)),
        ],
        out_specs=pl.BlockSpec((1, tq, LANES), lambda b, j, i: (b, i, j)),
        out_shape=jax.ShapeDtypeStruct((B, S, (H // 2) * LANES), BF16),
        compiler_params=pltpu.CompilerParams(
            dimension_semantics=("parallel", "parallel", "parallel"),
            vmem_limit_bytes=VMEM_LIMIT),
        name="attn",
    )(q, k, v)


def _mix_kernel(up_ref, um_ref, un_ref, ya_ref, x_ref, wpool_ref, pscale_ref,
                wout_ref, gffn_ref, x1_ref, h2_ref, ext_ref, *, seq_len):
    tile = um_ref.shape[1]
    i = pl.program_id(1)
    n_tiles = pl.num_programs(1)
    H8 = HALO_F32
    ext_ref[0:H8, :] = jnp.where(i > 0, up_ref[0], 0.0)
    ext_ref[H8:H8 + tile, :] = um_ref[0]
    ext_ref[H8 + tile:, :] = jnp.where(i < n_tiles - 1, un_ref[0], 0.0)
    pos = i * tile + lax.broadcasted_iota(jnp.int32, (tile, 1), 0)
    parts = []
    for g, w in enumerate(POOL_WINDOWS):
        cs = slice(g * POOL_GROUP_DIM, (g + 1) * POOL_GROUP_DIM)
        half = w // 2
        acc = ext_ref[H8 - half:H8 - half + tile, cs]
        for j in range(-half + 1, half):
            acc = acc + ext_ref[H8 + j:H8 + j + tile, cs]
        lo = jnp.maximum(pos - half, 0)
        hi = jnp.minimum(pos + half, seq_len)
        cnt = (hi - lo).astype(F32)
        pooled = acc / cnt - ext_ref[H8:H8 + tile, cs]
        pm = _dot(pooled.astype(BF16), wpool_ref[g]) * pscale_ref[:, cs]
        parts.append(pm.astype(BF16))
    mixed = jnp.concatenate(parts + [ya_ref[0]], axis=-1)
    x1 = x_ref[0] + _dot(mixed, wout_ref[...])
    x1_ref[0] = x1
    h2_ref[0] = _rms(x1, gffn_ref[...]).astype(BF16)


def _mix_call(u, ya, x, p, tile):
    B, S, _ = x.shape
    grid = (B, S // tile)
    tb = tile // HALO_F32
    nb = S // HALO_F32
    const2 = lambda b, i: (0, 0)
    return pl.pallas_call(
        functools.partial(_mix_kernel, seq_len=S),
        grid=grid,
        in_specs=[
            pl.BlockSpec((1, HALO_F32, POOL_WIDTH),
                         lambda b, i: (b, jnp.maximum(i * tb - 1, 0), 0)),
            pl.BlockSpec((1, tile, POOL_WIDTH), lambda b, i: (b, i, 0)),
            pl.BlockSpec((1, HALO_F32, POOL_WIDTH),
                         lambda b, i: (b, jnp.minimum((i + 1) * tb, nb - 1), 0)),
            pl.BlockSpec((1, tile, POOL_WIDTH), lambda b, i: (b, i, 0)),
            pl.BlockSpec((1, tile, D_MODEL), lambda b, i: (b, i, 0)),
            pl.BlockSpec(p['w_pool'].shape, lambda b, i: (0, 0, 0)),
            pl.BlockSpec((1, POOL_WIDTH), const2),
            pl.BlockSpec((D_MODEL, D_MODEL), const2),
            pl.BlockSpec((1, D_MODEL), const2),
        ],
        out_specs=[
            pl.BlockSpec((1, tile, D_MODEL), lambda b, i: (b, i, 0)),
            pl.BlockSpec((1, tile, D_MODEL), lambda b, i: (b, i, 0)),
        ],
        out_shape=[
            jax.ShapeDtypeStruct((B, S, D_MODEL), F32),
            jax.ShapeDtypeStruct((B, S, D_MODEL), BF16),
        ],
        scratch_shapes=[pltpu.VMEM((tile + 2 * HALO_F32, POOL_WIDTH), F32)],
        compiler_params=pltpu.CompilerParams(
            dimension_semantics=("parallel", "parallel"),
            vmem_limit_bytes=VMEM_LIMIT),
        name="mix_out",
    )(u, u, u, ya, x, p['w_pool'], p['pool_scale'], p['w_out'], p['g_ffn'])


def _ffn_kernel(hp_ref, hm_ref, hn_ref, x1_ref, wup_ref, cw_ref, cb_ref,
                wdn_ref, gfin_ref, y_ref, hext_ref, u_ref, acc_ref):
    tile = hm_ref.shape[1]
    i = pl.program_id(1)
    n_tiles = pl.num_programs(1)
    H16 = HALO_BF16
    hext_ref[0:H16, :] = jnp.where(i > 0, hp_ref[0], jnp.zeros_like(hp_ref[0]))
    hext_ref[H16:H16 + tile, :] = hm_ref[0]
    hext_ref[H16 + tile:, :] = jnp.where(i < n_tiles - 1, hn_ref[0],
                                         jnp.zeros_like(hn_ref[0]))
    acc_ref[...] = jnp.zeros_like(acc_ref)

    def chunk(c, carry):
        u_ref[...] = _dot(hext_ref[...], wup_ref[c])
        cw = cw_ref[c]
        conv = (cb_ref[c]
                + u_ref[H16 - 1:H16 - 1 + tile, :] * cw[0:1, :]
                + u_ref[H16:H16 + tile, :] * cw[1:2, :]
                + u_ref[H16 + 1:H16 + 1 + tile, :] * cw[2:3, :])
        gate = conv[:, :F_CHUNK]
        val = conv[:, F_CHUNK:]
        act = (gate * jax.nn.sigmoid(gate) * val).astype(BF16)
        acc_ref[...] += _dot(act, wdn_ref[c])
        return carry

    lax.fori_loop(0, N_F_CHUNKS, chunk, 0)
    y_ref[0] = _rms(x1_ref[0] + acc_ref[...], gfin_ref[...])


def _ffn_call(h2, x1, p, tile):
    B, S, _ = x1.shape
    grid = (B, S // tile)
    tb = tile // HALO_BF16
    nb = S // HALO_BF16
    const2 = lambda b, i: (0, 0)
    const3 = lambda b, i: (0, 0, 0)
    resident = dict(pipeline_mode=pl.Buffered(1))
    return pl.pallas_call(
        _ffn_kernel,
        grid=grid,
        in_specs=[
            pl.BlockSpec((1, HALO_BF16, D_MODEL),
                         lambda b, i: (b, jnp.maximum(i * tb - 1, 0), 0)),
            pl.BlockSpec((1, tile, D_MODEL), lambda b, i: (b, i, 0)),
            pl.BlockSpec((1, HALO_BF16, D_MODEL),
                         lambda b, i: (b, jnp.minimum((i + 1) * tb, nb - 1), 0)),
            pl.BlockSpec((1, tile, D_MODEL), lambda b, i: (b, i, 0)),
            pl.BlockSpec(p['w_up'].shape, const3, **resident),
            pl.BlockSpec(p['conv_w'].shape, const3, **resident),
            pl.BlockSpec(p['conv_b'].shape, const3, **resident),
            pl.BlockSpec(p['w_down'].shape, const3, **resident),
            pl.BlockSpec((1, D_MODEL), const2),
        ],
        out_specs=pl.BlockSpec((1, tile, D_MODEL), lambda b, i: (b, i, 0)),
        out_shape=jax.ShapeDtypeStruct((B, S, D_MODEL), F32),
        scratch_shapes=[
            pltpu.VMEM((tile + 2 * HALO_BF16, D_MODEL), BF16),
            pltpu.VMEM((tile + 2 * HALO_BF16, 2 * F_CHUNK), F32),
            pltpu.VMEM((tile, D_MODEL), F32),
        ],
        compiler_params=pltpu.CompilerParams(
            dimension_semantics=("parallel", "parallel"),
            vmem_limit_bytes=VMEM_LIMIT),
        name="conv_ffn",
    )(h2, h2, h2, x1, p['w_up'], p['conv_w'], p['conv_b'], p['w_down'], p['g_fin'])


def _prep_params(norm_mix_g, w_in, q_norm_g, w_uq, kv_norm_g, w_ukv, w_pool,
                 pool_scale, w_out, norm_ffn_g, w_up, conv_w, conv_b, w_down,
                 final_norm_g):
    R = HALF_ROPE
    nope = QK_NOPE_DIM
    zeros = lambda r, c: jnp.zeros((r, c), F32)

    o = POOL_WIDTH + Q_LORA_RANK + KV_LORA_RANK
    kx1 = w_in[:, o:o + R]
    kx2 = w_in[:, o + R:o + 2 * R]
    pad_l = zeros(D_MODEL, nope)
    pad_r = zeros(D_MODEL, LANES - nope - 2 * R)
    w_in_p = jnp.concatenate(
        [w_in[:, :o], pad_l, kx1, kx2, pad_r, pad_l, kx2, kx1, pad_r], axis=1)

    qa, qb, wk, wv, vones = [], [], [], [], []
    for h in range(N_HEADS):
        c0 = h * QK_HEAD_DIM
        qn_ = w_uq[:, c0:c0 + nope]
        qx1 = w_uq[:, c0 + nope:c0 + nope + R]
        qx2 = w_uq[:, c0 + nope + R:c0 + nope + 2 * R]
        zq = zeros(Q_LORA_RANK, LANES - nope - 2 * R)
        qa += [qn_, qx1, qx2, zq]
        qb += [zeros(Q_LORA_RANK, nope), qx2, qx1, zq]
        k0 = h * (nope + V_HEAD_DIM)
        zk = zeros(KV_LORA_RANK, LANES - nope)
        wk += [w_ukv[:, k0:k0 + nope], zk]
        vcols = w_ukv[:, k0 + nope:k0 + nope + V_HEAD_DIM]
        zv = zeros(KV_LORA_RANK, LANES - V_HEAD_DIM)
        one_lane = V_HEAD_DIM if h % 2 == 0 else 0
        wv += [vcols, zv] if h % 2 == 0 else [zv, vcols]
        vones.append((jnp.arange(LANES) == one_lane).astype(F32)[None, :])

    nc = N_F_CHUNKS
    gate_w = w_up[:, :D_FF].reshape(D_MODEL, nc, F_CHUNK)
    val_w = w_up[:, D_FF:].reshape(D_MODEL, nc, F_CHUNK)
    w_up_c = jnp.concatenate([gate_w, val_w], axis=2).transpose(1, 0, 2)
    cw_c = jnp.concatenate([conv_w[:, :D_FF].reshape(3, nc, F_CHUNK),
                            conv_w[:, D_FF:].reshape(3, nc, F_CHUNK)],
                           axis=2).transpose(1, 0, 2)
    cb_c = jnp.concatenate([conv_b[:D_FF].reshape(nc, 1, F_CHUNK),
                            conv_b[D_FF:].reshape(nc, 1, F_CHUNK)], axis=2)
    return dict(
        g_mix=norm_mix_g[None, :], w_in=w_in_p.astype(BF16),
        g_q=q_norm_g[None, :], g_kv=kv_norm_g[None, :],
        w_qa=jnp.concatenate(qa, axis=1).astype(BF16),
        w_qb=jnp.concatenate(qb, axis=1).astype(BF16),
        w_k=jnp.concatenate(wk, axis=1).astype(BF16),
        w_v=jnp.concatenate(wv, axis=1).astype(BF16),
        vones=jnp.concatenate(vones, axis=1),
        w_pool=w_pool.astype(BF16), pool_scale=pool_scale[None, :],
        w_out=w_out.astype(BF16), g_ffn=norm_ffn_g[None, :],
        w_up=w_up_c.astype(BF16), conv_w=cw_c, conv_b=cb_c,
        w_down=w_down.reshape(nc, F_CHUNK, D_MODEL).astype(BF16),
        g_fin=final_norm_g[None, :],
    )


def _rope_tables(seq):
    inv_freq = ROPE_THETA ** (-jnp.arange(0, QK_ROPE_DIM, 2, dtype=F32) / QK_ROPE_DIM)
    ang = jnp.arange(seq, dtype=F32)[:, None] * inv_freq[None, :]
    c, s = jnp.cos(ang), jnp.sin(ang)
    ones_l = jnp.ones((seq, QK_NOPE_DIM), F32)
    ones_r = jnp.ones((seq, LANES - QK_NOPE_DIM - QK_ROPE_DIM), F32)
    cosf = jnp.concatenate([ones_l, c, c, ones_r], axis=1)
    sinf = jnp.concatenate([0 * ones_l, -s, s, 0 * ones_r], axis=1)
    return cosf, sinf


def _encoder(x, p):
    S = x.shape[1]
    cosf, sinf = _rope_tables(S)
    u, q, k, v = _pre_call(x, p, cosf, sinf, tile=512)
    ya = _attn_call(q, k, v, tq=min(S, 2048), q_sub=256, k_chunk=min(S, 2048))
    x1, h2 = _mix_call(u, ya, x, p, tile=512)
    return _ffn_call(h2, x1, p, tile=512)


def kernel(x_prompt, x_sample, norm_mix_g, w_in, q_norm_g, w_uq, kv_norm_g, w_ukv, w_pool, pool_scale, w_out, norm_ffn_g, w_up, conv_w, conv_b, w_down, final_norm_g):
    assert norm_mix_g.shape[0] == 1
    p = _prep_params(norm_mix_g[0], w_in[0], q_norm_g[0], w_uq[0], kv_norm_g[0],
                     w_ukv[0], w_pool[0], pool_scale[0], w_out[0], norm_ffn_g[0],
                     w_up[0], conv_w[0], conv_b[0], w_down[0], final_norm_g)
    return (_encoder(x_prompt, p), _encoder(x_sample, p))
```

```python
import functools
import math

import jax
import jax.numpy as jnp
from jax import lax
from jax.experimental import pallas as pl
from jax.experimental.pallas import tpu as pltpu

D_MODEL = 1024
POOL_WINDOWS = (2, 4, 8, 16)
POOL_WIDTH = 512
POOL_GROUP_DIM = 128
N_HEADS = 8
QK_NOPE_DIM = 64
QK_ROPE_DIM = 32
QK_HEAD_DIM = 96
V_HEAD_DIM = 64
Q_LORA_RANK = 256
KV_LORA_RANK = 128
ROPE_THETA = 10000.0
D_FF = 2816
EPS = 1e-6

LANES = 128
HEAD_PAD = LANES
HALF_ROPE = QK_ROPE_DIM // 2
F_CHUNK = 256
N_F_CHUNKS = D_FF // F_CHUNK
HALO_F32 = 8
HALO_BF16 = 16
VMEM_LIMIT = 48 * 1024 * 1024

BF16 = jnp.bfloat16
F32 = jnp.float32


def _rms(x, g):
    ms = jnp.mean(x * x, axis=-1, keepdims=True)
    return x * lax.rsqrt(ms + EPS) * g


def _dot(a, b):
    return jnp.dot(a, b, preferred_element_type=F32)


def _pre_kernel(x_ref, gmix_ref, win_ref, gq_ref, gkv_ref, wqa_ref, wqb_ref,
                wk_ref, wv_ref, vones_ref, cos_ref, sin_ref,
                u_ref, q_ref, k_ref, v_ref, *, q_scale):
    x = x_ref[0]
    h = _rms(x, gmix_ref[...]).astype(BF16)
    z = _dot(h, win_ref[...])
    u_ref[0] = z[:, :POOL_WIDTH]
    o = POOL_WIDTH
    qn = _rms(z[:, o:o + Q_LORA_RANK], gq_ref[...]).astype(BF16)
    o += Q_LORA_RANK
    kvn = _rms(z[:, o:o + KV_LORA_RANK], gkv_ref[...]).astype(BF16)
    o += KV_LORA_RANK
    cos = cos_ref[...]
    sin = sin_ref[...]
    kr = z[:, o:o + LANES] * cos + z[:, o + LANES:o + 2 * LANES] * sin
    qa = _dot(qn, wqa_ref[...])
    qb = _dot(qn, wqb_ref[...])
    kk = _dot(kvn, wk_ref[...])
    vv = _dot(kvn, wv_ref[...]) + vones_ref[...]
    for hd in range(N_HEADS):
        sl = slice(hd * HEAD_PAD, (hd + 1) * HEAD_PAD)
        q_ref[0, hd] = ((qa[:, sl] * cos + qb[:, sl] * sin) * q_scale).astype(BF16)
        k_ref[0, hd] = (kk[:, sl] + kr).astype(BF16)
        v_ref[0, hd] = vv[:, sl].astype(BF16)


def _pre_call(x, p, cosf, sinf, tile):
    B, S, _ = x.shape
    grid = (B, S // tile)
    const2 = lambda b, i: (0, 0)
    q_scale = math.log2(math.e) / math.sqrt(QK_HEAD_DIM)
    head_shape = jax.ShapeDtypeStruct((B, N_HEADS, S, HEAD_PAD), BF16)
    head_spec = pl.BlockSpec((1, N_HEADS, tile, HEAD_PAD), lambda b, i: (b, 0, i, 0))
    return pl.pallas_call(
        functools.partial(_pre_kernel, q_scale=q_scale),
        grid=grid,
        in_specs=[
            pl.BlockSpec((1, tile, D_MODEL), lambda b, i: (b, i, 0)),
            pl.BlockSpec((1, D_MODEL), const2),
            pl.BlockSpec(p['w_in'].shape, const2),
            pl.BlockSpec((1, Q_LORA_RANK), const2),
            pl.BlockSpec((1, KV_LORA_RANK), const2),
            pl.BlockSpec(p['w_qa'].shape, const2),
            pl.BlockSpec(p['w_qb'].shape, const2),
            pl.BlockSpec(p['w_k'].shape, const2),
            pl.BlockSpec(p['w_v'].shape, const2),
            pl.BlockSpec(p['vones'].shape, const2),
            pl.BlockSpec((tile, LANES), lambda b, i: (i, 0)),
            pl.BlockSpec((tile, LANES), lambda b, i: (i, 0)),
        ],
        out_specs=[
            pl.BlockSpec((1, tile, POOL_WIDTH), lambda b, i: (b, i, 0)),
            head_spec, head_spec, head_spec,
        ],
        out_shape=[
            jax.ShapeDtypeStruct((B, S, POOL_WIDTH), F32),
            head_shape, head_shape, head_shape,
        ],
        compiler_params=pltpu.CompilerParams(
            dimension_semantics=("parallel", "parallel"),
            vmem_limit_bytes=VMEM_LIMIT),
        name="pre_proj",
    )(x, p['g_mix'], p['w_in'], p['g_q'], p['g_kv'], p['w_qa'], p['w_qb'],
      p['w_k'], p['w_v'], p['vones'], cosf, sinf)


def _attn_kernel(q_ref, k_ref, v_ref, o_ref, *, q_sub, k_chunk):
    tq = q_ref.shape[2]
    S = k_ref.shape[2]
    n_chunks = S // k_chunk
    lane = lax.broadcasted_iota(jnp.int32, (q_sub, LANES), 1)

    def attend(hd, row0):
        q = q_ref[0, hd, pl.ds(row0, q_sub), :]
        m = jnp.full((q_sub, 1), -1e30, F32)
        acc = jnp.zeros((q_sub, LANES), F32)
        for c in range(n_chunks):
            k = k_ref[0, hd, pl.ds(c * k_chunk, k_chunk), :]
            v = v_ref[0, hd, pl.ds(c * k_chunk, k_chunk), :]
            s = lax.dot_general(q, k, (((1,), (1,)), ((), ())),
                                preferred_element_type=F32)
            m_new = jnp.maximum(m, jnp.max(s, axis=-1, keepdims=True))
            pr = jnp.exp2(s - m_new).astype(BF16)
            pv = _dot(pr, v)
            if c == 0:
                acc = pv
            else:
                acc = acc * jnp.exp2(m - m_new) + pv
            m = m_new
        return acc

    def body(r, carry):
        row0 = pl.multiple_of(r * q_sub, q_sub)
        a0 = attend(0, row0)
        a1 = attend(1, row0)
        l0 = a0[:, V_HEAD_DIM:V_HEAD_DIM + 1]
        l1 = a1[:, 0:1]
        out = jnp.where(lane < V_HEAD_DIM, a0 / l0, a1 / l1)
        o_ref[0, pl.ds(row0, q_sub), :] = out.astype(o_ref.dtype)
        return carry

    lax.fori_loop(0, tq // q_sub, body, 0, unroll=2)


def _attn_call(q, k, v, tq, q_sub, k_chunk):
    B, H, S, _ = q.shape
    grid = (B, H // 2, S // tq)
    return pl.pallas_call(
        functools.partial(_attn_kernel, q_sub=q_sub, k_chunk=k_chunk),
        grid=grid,
        in_specs=[
            pl.BlockSpec((1, 2, tq, HEAD_PAD), lambda b, j, i: (b, j, i, 0)),
            pl.BlockSpec((1, 2, S, HEAD_PAD), lambda b, j, i: (b, j, 0, 0)),
            pl.BlockSpec((1, 2, S, HEAD_PAD), lambda b, j, i: (b, j, 0, 0)),
        ],
        out_specs=pl.BlockSpec((1, tq, LANES), lambda b, j, i: (b, i, j)),
        out_shape=jax.ShapeDtypeStruct((B, S, (H // 2) * LANES), BF16),
        compiler_params=pltpu.CompilerParams(
            dimension_semantics=("parallel", "parallel", "parallel"),
            vmem_limit_bytes=VMEM_LIMIT),
        name="attn",
    )(q, k, v)


def _mix_kernel(up_ref, um_ref, un_ref, ya_ref, x_ref, wpool_ref, pscale_ref,
                wout_ref, gffn_ref, x1_ref, h2_ref, ext_ref, *, seq_len):
    tile = um_ref.shape[1]
    i = pl.program_id(1)
    n_tiles = pl.num_programs(1)
    H8 = HALO_F32
    ext_ref[0:H8, :] = jnp.where(i > 0, up_ref[0], 0.0)
    ext_ref[H8:H8 + tile, :] = um_ref[0]
    ext_ref[H8 + tile:, :] = jnp.where(i < n_tiles - 1, un_ref[0], 0.0)
    pos = i * tile + lax.broadcasted_iota(jnp.int32, (tile, 1), 0)
    parts = []
    for g, w in enumerate(POOL_WINDOWS):
        cs = slice(g * POOL_GROUP_DIM, (g + 1) * POOL_GROUP_DIM)
        half = w // 2
        acc = ext_ref[H8 - half:H8 - half + tile, cs]
        for j in range(-half + 1, half):
            acc = acc + ext_ref[H8 + j:H8 + j + tile, cs]
        lo = jnp.maximum(pos - half, 0)
        hi = jnp.minimum(pos + half, seq_len)
        cnt = (hi - lo).astype(F32)
        pooled = acc / cnt - ext_ref[H8:H8 + tile, cs]
        pm = _dot(pooled.astype(BF16), wpool_ref[g]) * pscale_ref[:, cs]
        parts.append(pm.astype(BF16))
    mixed = jnp.concatenate(parts + [ya_ref[0]], axis=-1)
    x1 = x_ref[0] + _dot(mixed, wout_ref[...])
    x1_ref[0] = x1
    h2_ref[0] = _rms(x1, gffn_ref[...]).astype(BF16)


def _mix_call(u, ya, x, p, tile):
    B, S, _ = x.shape
    grid = (B, S // tile)
    tb = tile // HALO_F32
    nb = S // HALO_F32
    const2 = lambda b, i: (0, 0)
    return pl.pallas_call(
        functools.partial(_mix_kernel, seq_len=S),
        grid=grid,
        in_specs=[
            pl.BlockSpec((1, HALO_F32, POOL_WIDTH),
                         lambda b, i: (b, jnp.maximum(i * tb - 1, 0), 0)),
            pl.BlockSpec((1, tile, POOL_WIDTH), lambda b, i: (b, i, 0)),
            pl.BlockSpec((1, HALO_F32, POOL_WIDTH),
                         lambda b, i: (b, jnp.minimum((i + 1) * tb, nb - 1), 0)),
            pl.BlockSpec((1, tile, POOL_WIDTH), lambda b, i: (b, i, 0)),
            pl.BlockSpec((1, tile, D_MODEL), lambda b, i: (b, i, 0)),
            pl.BlockSpec(p['w_pool'].shape, lambda b, i: (0, 0, 0)),
            pl.BlockSpec((1, POOL_WIDTH), const2),
            pl.BlockSpec((D_MODEL, D_MODEL), const2),
            pl.BlockSpec((1, D_MODEL), const2),
        ],
        out_specs=[
            pl.BlockSpec((1, tile, D_MODEL), lambda b, i: (b, i, 0)),
            pl.BlockSpec((1, tile, D_MODEL), lambda b, i: (b, i, 0)),
        ],
        out_shape=[
            jax.ShapeDtypeStruct((B, S, D_MODEL), F32),
            jax.ShapeDtypeStruct((B, S, D_MODEL), BF16),
        ],
        scratch_shapes=[pltpu.VMEM((tile + 2 * HALO_F32, POOL_WIDTH), F32)],
        compiler_params=pltpu.CompilerParams(
            dimension_semantics=("parallel", "parallel"),
            vmem_limit_bytes=VMEM_LIMIT),
        name="mix_out",
    )(u, u, u, ya, x, p['w_pool'], p['pool_scale'], p['w_out'], p['g_ffn'])


def _ffn_kernel(hp_ref, hm_ref, hn_ref, x1_ref, wup_ref, cw_ref, cb_ref,
                wdn_ref, gfin_ref, y_ref, hext_ref, ua_ref, ub_ref, acc_ref):
    tile = hm_ref.shape[1]
    i = pl.program_id(1)
    n_tiles = pl.num_programs(1)
    H16 = HALO_BF16
    hext_ref[0:H16, :] = jnp.where(i > 0, hp_ref[0], jnp.zeros_like(hp_ref[0]))
    hext_ref[H16:H16 + tile, :] = hm_ref[0]
    hext_ref[H16 + tile:, :] = jnp.where(i < n_tiles - 1, hn_ref[0],
                                         jnp.zeros_like(hn_ref[0]))
    acc_ref[...] = jnp.zeros_like(acc_ref)

    def up_proj(c, u_ref):
        u_ref[...] = _dot(hext_ref[...], wup_ref[c])

    def conv_down(c, u_ref):
        cw = cw_ref[c]
        conv = (cb_ref[c]
                + u_ref[H16 - 1:H16 - 1 + tile, :] * cw[0:1, :]
                + u_ref[H16:H16 + tile, :] * cw[1:2, :]
                + u_ref[H16 + 1:H16 + 1 + tile, :] * cw[2:3, :])
        gate = conv[:, :F_CHUNK]
        val = conv[:, F_CHUNK:]
        act = (gate * jax.nn.sigmoid(gate) * val).astype(BF16)
        acc_ref[...] += _dot(act, wdn_ref[c])

    up_proj(0, ua_ref)

    def chunk_pair(k, carry):
        c = 2 * k
        up_proj(c + 1, ub_ref)
        conv_down(c, ua_ref)
        up_proj(c + 2, ua_ref)
        conv_down(c + 1, ub_ref)
        return carry

    assert N_F_CHUNKS % 2 == 1
    lax.fori_loop(0, N_F_CHUNKS // 2, chunk_pair, 0)
    conv_down(N_F_CHUNKS - 1, ua_ref)
    y_ref[0] = _rms(x1_ref[0] + acc_ref[...], gfin_ref[...])


def _ffn_call(h2, x1, p, tile):
    B, S, _ = x1.shape
    grid = (B, S // tile)
    tb = tile // HALO_BF16
    nb = S // HALO_BF16
    const2 = lambda b, i: (0, 0)
    const3 = lambda b, i: (0, 0, 0)
    resident = dict(pipeline_mode=pl.Buffered(1))
    return pl.pallas_call(
        _ffn_kernel,
        grid=grid,
        in_specs=[
            pl.BlockSpec((1, HALO_BF16, D_MODEL),
                         lambda b, i: (b, jnp.maximum(i * tb - 1, 0), 0)),
            pl.BlockSpec((1, tile, D_MODEL), lambda b, i: (b, i, 0)),
            pl.BlockSpec((1, HALO_BF16, D_MODEL),
                         lambda b, i: (b, jnp.minimum((i + 1) * tb, nb - 1), 0)),
            pl.BlockSpec((1, tile, D_MODEL), lambda b, i: (b, i, 0)),
            pl.BlockSpec(p['w_up'].shape, const3, **resident),
            pl.BlockSpec(p['conv_w'].shape, const3, **resident),
            pl.BlockSpec(p['conv_b'].shape, const3, **resident),
            pl.BlockSpec(p['w_down'].shape, const3, **resident),
            pl.BlockSpec((1, D_MODEL), const2),
        ],
        out_specs=pl.BlockSpec((1, tile, D_MODEL), lambda b, i: (b, i, 0)),
        out_shape=jax.ShapeDtypeStruct((B, S, D_MODEL), F32),
        scratch_shapes=[
            pltpu.VMEM((tile + 2 * HALO_BF16, D_MODEL), BF16),
            pltpu.VMEM((tile + 2 * HALO_BF16, 2 * F_CHUNK), F32),
            pltpu.VMEM((tile + 2 * HALO_BF16, 2 * F_CHUNK), F32),
            pltpu.VMEM((tile, D_MODEL), F32),
        ],
        compiler_params=pltpu.CompilerParams(
            dimension_semantics=("parallel", "parallel"),
            vmem_limit_bytes=VMEM_LIMIT),
        name="conv_ffn",
    )(h2, h2, h2, x1, p['w_up'], p['conv_w'], p['conv_b'], p['w_down'], p['g_fin'])


def _prep_params(norm_mix_g, w_in, q_norm_g, w_uq, kv_norm_g, w_ukv, w_pool,
                 pool_scale, w_out, norm_ffn_g, w_up, conv_w, conv_b, w_down,
                 final_norm_g):
    R = HALF_ROPE
    nope = QK_NOPE_DIM
    zeros = lambda r, c: jnp.zeros((r, c), F32)

    o = POOL_WIDTH + Q_LORA_RANK + KV_LORA_RANK
    kx1 = w_in[:, o:o + R]
    kx2 = w_in[:, o + R:o + 2 * R]
    pad_l = zeros(D_MODEL, nope)
    pad_r = zeros(D_MODEL, LANES - nope - 2 * R)
    w_in_p = jnp.concatenate(
        [w_in[:, :o], pad_l, kx1, kx2, pad_r, pad_l, kx2, kx1, pad_r], axis=1)

    qa, qb, wk, wv, vones = [], [], [], [], []
    for h in range(N_HEADS):
        c0 = h * QK_HEAD_DIM
        qn_ = w_uq[:, c0:c0 + nope]
        qx1 = w_uq[:, c0 + nope:c0 + nope + R]
        qx2 = w_uq[:, c0 + nope + R:c0 + nope + 2 * R]
        zq = zeros(Q_LORA_RANK, LANES - nope - 2 * R)
        qa += [qn_, qx1, qx2, zq]
        qb += [zeros(Q_LORA_RANK, nope), qx2, qx1, zq]
        k0 = h * (nope + V_HEAD_DIM)
        zk = zeros(KV_LORA_RANK, LANES - nope)
        wk += [w_ukv[:, k0:k0 + nope], zk]
        vcols = w_ukv[:, k0 + nope:k0 + nope + V_HEAD_DIM]
        zv = zeros(KV_LORA_RANK, LANES - V_HEAD_DIM)
        one_lane = V_HEAD_DIM if h % 2 == 0 else 0
        wv += [vcols, zv] if h % 2 == 0 else [zv, vcols]
        vones.append((jnp.arange(LANES) == one_lane).astype(F32)[None, :])

    nc = N_F_CHUNKS
    gate_w = w_up[:, :D_FF].reshape(D_MODEL, nc, F_CHUNK)
    val_w = w_up[:, D_FF:].reshape(D_MODEL, nc, F_CHUNK)
    w_up_c = jnp.concatenate([gate_w, val_w], axis=2).transpose(1, 0, 2)
    cw_c = jnp.concatenate([conv_w[:, :D_FF].reshape(3, nc, F_CHUNK),
                            conv_w[:, D_FF:].reshape(3, nc, F_CHUNK)],
                           axis=2).transpose(1, 0, 2)
    cb_c = jnp.concatenate([conv_b[:D_FF].reshape(nc, 1, F_CHUNK),
                            conv_b[D_FF:].reshape(nc, 1, F_CHUNK)], axis=2)
    return dict(
        g_mix=norm_mix_g[None, :], w_in=w_in_p.astype(BF16),
        g_q=q_norm_g[None, :], g_kv=kv_norm_g[None, :],
        w_qa=jnp.concatenate(qa, axis=1).astype(BF16),
        w_qb=jnp.concatenate(qb, axis=1).astype(BF16),
        w_k=jnp.concatenate(wk, axis=1).astype(BF16),
        w_v=jnp.concatenate(wv, axis=1).astype(BF16),
        vones=jnp.concatenate(vones, axis=1),
        w_pool=w_pool.astype(BF16), pool_scale=pool_scale[None, :],
        w_out=w_out.astype(BF16), g_ffn=norm_ffn_g[None, :],
        w_up=w_up_c.astype(BF16), conv_w=cw_c, conv_b=cb_c,
        w_down=w_down.reshape(nc, F_CHUNK, D_MODEL).astype(BF16),
        g_fin=final_norm_g[None, :],
    )


def _rope_tables(seq):
    inv_freq = ROPE_THETA ** (-jnp.arange(0, QK_ROPE_DIM, 2, dtype=F32) / QK_ROPE_DIM)
    ang = jnp.arange(seq, dtype=F32)[:, None] * inv_freq[None, :]
    c, s = jnp.cos(ang), jnp.sin(ang)
    ones_l = jnp.ones((seq, QK_NOPE_DIM), F32)
    ones_r = jnp.ones((seq, LANES - QK_NOPE_DIM - QK_ROPE_DIM), F32)
    cosf = jnp.concatenate([ones_l, c, c, ones_r], axis=1)
    sinf = jnp.concatenate([0 * ones_l, -s, s, 0 * ones_r], axis=1)
    return cosf, sinf


def _encoder(x, p):
    S = x.shape[1]
    cosf, sinf = _rope_tables(S)
    u, q, k, v = _pre_call(x, p, cosf, sinf, tile=512)
    ya = _attn_call(q, k, v, tq=min(S, 2048), q_sub=512, k_chunk=min(2048, S // 2))
    x1, h2 = _mix_call(u, ya, x, p, tile=512)
    return _ffn_call(h2, x1, p, tile=512)


def kernel(x_prompt, x_sample, norm_mix_g, w_in, q_norm_g, w_uq, kv_norm_g, w_ukv, w_pool, pool_scale, w_out, norm_ffn_g, w_up, conv_w, conv_b, w_down, final_norm_g):
    assert norm_mix_g.shape[0] == 1
    p = _prep_params(norm_mix_g[0], w_in[0], q_norm_g[0], w_uq[0], kv_norm_g[0],
                     w_ukv[0], w_pool[0], pool_scale[0], w_out[0], norm_ffn_g[0],
                     w_up[0], conv_w[0], conv_b[0], w_down[0], final_norm_g)
    return (_encoder(x_prompt, p), _encoder(x_sample, p))
```

```python
import functools
import math

import jax
import jax.numpy as jnp
from jax import lax
from jax.experimental import pallas as pl
from jax.experimental.pallas import tpu as pltpu

D_MODEL = 1024
POOL_WINDOWS = (2, 4, 8, 16)
POOL_WIDTH = 512
POOL_GROUP_DIM = 128
N_HEADS = 8
QK_NOPE_DIM = 64
QK_ROPE_DIM = 32
QK_HEAD_DIM = 96
V_HEAD_DIM = 64
Q_LORA_RANK = 256
KV_LORA_RANK = 128
ROPE_THETA = 10000.0
D_FF = 2816
EPS = 1e-6

LANES = 128
HEAD_PAD = LANES
HALF_ROPE = QK_ROPE_DIM // 2
F_CHUNK = 256
N_F_CHUNKS = D_FF // F_CHUNK
HALO_F32 = 8
HALO_BF16 = 16
VMEM_LIMIT = 48 * 1024 * 1024
FFN_VMEM_LIMIT = 56 * 1024 * 1024

BF16 = jnp.bfloat16
F32 = jnp.float32


def _rms(x, g):
    ms = jnp.mean(x * x, axis=-1, keepdims=True)
    return x * lax.rsqrt(ms + EPS) * g


def _dot(a, b):
    return jnp.dot(a, b, preferred_element_type=F32)


def _pre_kernel(x_ref, gmix_ref, win_ref, gq_ref, gkv_ref, wqa_ref, wqb_ref,
                wkv_ref, rope_ref, u_ref, q_ref, k_ref, v_ref, *, q_scale):
    x = x_ref[0]
    tile = x.shape[0]
    h = _rms(x, gmix_ref[...]).astype(BF16)
    z = _dot(h, win_ref[...])
    u_ref[0] = z[:, :POOL_WIDTH]
    o = POOL_WIDTH
    qn = _rms(z[:, o:o + Q_LORA_RANK], gq_ref[...]).astype(BF16)
    o += Q_LORA_RANK
    kvn = _rms(z[:, o:o + KV_LORA_RANK], gkv_ref[...]).astype(BF16)
    o += KV_LORA_RANK
    cos_e = rope_ref[:, 0 * LANES:1 * LANES]
    sin_e = rope_ref[:, 1 * LANES:2 * LANES]
    cos_o = rope_ref[:, 2 * LANES:3 * LANES]
    sin_o = rope_ref[:, 3 * LANES:4 * LANES]
    lane = lax.broadcasted_iota(jnp.int32, (tile, LANES), 1)
    low_half = lane < LANES // 2
    kblk = z[:, o:o + LANES]
    kr_o = kblk * cos_o + pltpu.roll(kblk, LANES - QK_ROPE_DIM, axis=1) * sin_o
    kr_o = jnp.where(lane < QK_NOPE_DIM + QK_ROPE_DIM, kr_o, 0.0)
    kr_e = pltpu.roll(kr_o, LANES // 2, axis=1)
    one_e = jnp.where(lane == V_HEAD_DIM, 1.0, 0.0)
    one_o = jnp.where(lane == 0, 1.0, 0.0)
    qa = _dot(qn, wqa_ref[...])
    qb = _dot(qn, wqb_ref[...])
    kv = _dot(kvn, wkv_ref[...])
    for hd in range(N_HEADS):
        sl = slice(hd * HEAD_PAD, (hd + 1) * HEAD_PAD)
        even = hd % 2 == 0
        cos, sin = (cos_e, sin_e) if even else (cos_o, sin_o)
        q_ref[0, hd] = ((qa[:, sl] * cos + qb[:, sl] * sin) * q_scale).astype(BF16)
        blk = kv[:, sl]
        if even:
            k_ref[0, hd] = jnp.where(low_half, kr_e, blk).astype(BF16)
            v_ref[0, hd] = jnp.where(low_half, blk, one_e).astype(BF16)
        else:
            k_ref[0, hd] = jnp.where(low_half, blk, kr_o).astype(BF16)
            v_ref[0, hd] = jnp.where(low_half, one_o, blk).astype(BF16)


def _pre_call(x, p, rope, tile):
    B, S, _ = x.shape
    grid = (B, S // tile)
    const2 = lambda b, i: (0, 0)
    q_scale = math.log2(math.e) / math.sqrt(QK_HEAD_DIM)
    head_shape = jax.ShapeDtypeStruct((B, N_HEADS, S, HEAD_PAD), BF16)
    head_spec = pl.BlockSpec((1, N_HEADS, tile, HEAD_PAD), lambda b, i: (b, 0, i, 0))
    return pl.pallas_call(
        functools.partial(_pre_kernel, q_scale=q_scale),
        grid=grid,
        in_specs=[
            pl.BlockSpec((1, tile, D_MODEL), lambda b, i: (b, i, 0)),
            pl.BlockSpec((1, D_MODEL), const2),
            pl.BlockSpec(p['w_in'].shape, const2),
            pl.BlockSpec((1, Q_LORA_RANK), const2),
            pl.BlockSpec((1, KV_LORA_RANK), const2),
            pl.BlockSpec(p['w_qa'].shape, const2),
            pl.BlockSpec(p['w_qb'].shape, const2),
            pl.BlockSpec(p['w_kv'].shape, const2),
            pl.BlockSpec((tile, 4 * LANES), lambda b, i: (i, 0)),
        ],
        out_specs=[
            pl.BlockSpec((1, tile, POOL_WIDTH), lambda b, i: (b, i, 0)),
            head_spec, head_spec, head_spec,
        ],
        out_shape=[
            jax.ShapeDtypeStruct((B, S, POOL_WIDTH), F32),
            head_shape, head_shape, head_shape,
        ],
        compiler_params=pltpu.CompilerParams(
            dimension_semantics=("parallel", "parallel"),
            vmem_limit_bytes=VMEM_LIMIT),
        name="pre_proj",
    )(x, p['g_mix'], p['w_in'], p['g_q'], p['g_kv'], p['w_qa'], p['w_qb'],
      p['w_kv'], rope)


def _attn_kernel(q_ref, k_ref, v_ref, o_ref, *, q_sub, k_chunk):
    tq = q_ref.shape[2]
    S = k_ref.shape[2]
    n_chunks = S // k_chunk
    lane = lax.broadcasted_iota(jnp.int32, (q_sub, LANES), 1)

    def attend(hd, row0):
        q = q_ref[0, hd, pl.ds(row0, q_sub), :]
        m = jnp.full((q_sub, 1), -1e30, F32)
        acc = jnp.zeros((q_sub, LANES), F32)
        for c in range(n_chunks):
            k = k_ref[0, hd, pl.ds(c * k_chunk, k_chunk), :]
            v = v_ref[0, hd, pl.ds(c * k_chunk, k_chunk), :]
            s = lax.dot_general(q, k, (((1,), (1,)), ((), ())),
                                preferred_element_type=F32)
            m_new = jnp.maximum(m, jnp.max(s, axis=-1, keepdims=True))
            pr = jnp.exp2(s - m_new).astype(BF16)
            pv = _dot(pr, v)
            if c == 0:
                acc = pv
            else:
                acc = acc * jnp.exp2(m - m_new) + pv
            m = m_new
        return acc

    def body(r, carry):
        row0 = pl.multiple_of(r * q_sub, q_sub)
        a0 = attend(0, row0)
        a1 = attend(1, row0)
        l0 = a0[:, V_HEAD_DIM:V_HEAD_DIM + 1]
        l1 = a1[:, 0:1]
        out = jnp.where(lane < V_HEAD_DIM, a0 / l0, a1 / l1)
        o_ref[0, pl.ds(row0, q_sub), :] = out.astype(o_ref.dtype)
        return carry

    lax.fori_loop(0, tq // q_sub, body, 0, unroll=2)


def _attn_call(q, k, v, tq, q_sub, k_chunk):
    B, H, S, _ = q.shape
    grid = (B, H // 2, S // tq)
    return pl.pallas_call(
        functools.partial(_attn_kernel, q_sub=q_sub, k_chunk=k_chunk),
        grid=grid,
        in_specs=[
            pl.BlockSpec((1, 2, tq, HEAD_PAD), lambda b, j, i: (b, j, i, 0)),
            pl.BlockSpec((1, 2, S, HEAD_PAD), lambda b, j, i: (b, j, 0, 0)),
            pl.BlockSpec((1, 2, S, HEAD_PAD), lambda b, j, i: (b, j, 0, 0)),
        ],
        out_specs=pl.BlockSpec((1, tq, LANES), lambda b, j, i: (b, i, j)),
        out_shape=jax.ShapeDtypeStruct((B, S, (H // 2) * LANES), BF16),
        compiler_params=pltpu.CompilerParams(
            dimension_semantics=("parallel", "parallel", "parallel"),
            vmem_limit_bytes=VMEM_LIMIT),
        name="attn",
    )(q, k, v)


def _mix_kernel(up_ref, um_ref, un_ref, ya_ref, x_ref, wpool_ref, pscale_ref,
                wout_ref, gffn_ref, x1_ref, h2_ref, ext_ref, s2_ref, s4_ref, s8_ref,
                *, seq_len):
    tile = um_ref.shape[1]
    i = pl.program_id(1)
    n_tiles = pl.num_programs(1)
    H8 = HALO_F32
    G = POOL_GROUP_DIM
    ext_ref[0:H8, :] = jnp.where(i > 0, up_ref[0], 0.0)
    ext_ref[H8:H8 + tile, :] = um_ref[0]
    ext_ref[H8 + tile:2 * H8 + tile, :] = jnp.where(i < n_tiles - 1, un_ref[0], 0.0)
    ext_ref[2 * H8 + tile:, :] = jnp.zeros((2 * H8, POOL_WIDTH), F32)
    n2, n4, n8 = tile + 3 * H8, tile + 2 * H8, tile + H8
    s2_ref[...] = ext_ref[0:n2, G:] + ext_ref[1:n2 + 1, G:]
    s4_ref[...] = s2_ref[0:n4, :] + s2_ref[2:n4 + 2, :]
    s8_ref[...] = s4_ref[0:n8, G:] + s4_ref[4:n8 + 4, G:]
    window_sums = (
        ext_ref[H8 - 1:H8 - 1 + tile, 0:G] + ext_ref[H8:H8 + tile, 0:G],
        s4_ref[H8 - 2:H8 - 2 + tile, 0:G],
        s8_ref[H8 - 4:H8 - 4 + tile, 0:G],
        s8_ref[0:tile, G:] + s8_ref[H8:H8 + tile, G:],
    )
    pos = i * tile + lax.broadcasted_iota(jnp.int32, (tile, 1), 0)
    parts = []
    for g, w in enumerate(POOL_WINDOWS):
        cs = slice(g * POOL_GROUP_DIM, (g + 1) * POOL_GROUP_DIM)
        half = w // 2
        acc = window_sums[g]
        lo = jnp.maximum(pos - half, 0)
        hi = jnp.minimum(pos + half, seq_len)
        cnt = (hi - lo).astype(F32)
        pooled = acc / cnt - ext_ref[H8:H8 + tile, cs]
        pm = _dot(pooled.astype(BF16), wpool_ref[g]) * pscale_ref[:, cs]
        parts.append(pm.astype(BF16))
    mixed = jnp.concatenate(parts + [ya_ref[0]], axis=-1)
    x1 = x_ref[0] + _dot(mixed, wout_ref[...])
    x1_ref[0] = x1
    h2_ref[0] = _rms(x1, gffn_ref[...]).astype(BF16)


def _mix_call(u, ya, x, p, tile):
    B, S, _ = x.shape
    grid = (B, S // tile)
    tb = tile // HALO_F32
    nb = S // HALO_F32
    const2 = lambda b, i: (0, 0)
    return pl.pallas_call(
        functools.partial(_mix_kernel, seq_len=S),
        grid=grid,
        in_specs=[
            pl.BlockSpec((1, HALO_F32, POOL_WIDTH),
                         lambda b, i: (b, jnp.maximum(i * tb - 1, 0), 0)),
            pl.BlockSpec((1, tile, POOL_WIDTH), lambda b, i: (b, i, 0)),
            pl.BlockSpec((1, HALO_F32, POOL_WIDTH),
                         lambda b, i: (b, jnp.minimum((i + 1) * tb, nb - 1), 0)),
            pl.BlockSpec((1, tile, POOL_WIDTH), lambda b, i: (b, i, 0)),
            pl.BlockSpec((1, tile, D_MODEL), lambda b, i: (b, i, 0)),
            pl.BlockSpec(p['w_pool'].shape, lambda b, i: (0, 0, 0)),
            pl.BlockSpec((1, POOL_WIDTH), const2),
            pl.BlockSpec((D_MODEL, D_MODEL), const2),
            pl.BlockSpec((1, D_MODEL), const2),
        ],
        out_specs=[
            pl.BlockSpec((1, tile, D_MODEL), lambda b, i: (b, i, 0)),
            pl.BlockSpec((1, tile, D_MODEL), lambda b, i: (b, i, 0)),
        ],
        out_shape=[
            jax.ShapeDtypeStruct((B, S, D_MODEL), F32),
            jax.ShapeDtypeStruct((B, S, D_MODEL), BF16),
        ],
        scratch_shapes=[
            pltpu.VMEM((tile + 4 * HALO_F32, POOL_WIDTH), F32),
            pltpu.VMEM((tile + 3 * HALO_F32, 3 * POOL_GROUP_DIM), F32),
            pltpu.VMEM((tile + 2 * HALO_F32, 3 * POOL_GROUP_DIM), F32),
            pltpu.VMEM((tile + HALO_F32, 2 * POOL_GROUP_DIM), F32),
        ],
        compiler_params=pltpu.CompilerParams(
            dimension_semantics=("parallel", "parallel"),
            vmem_limit_bytes=VMEM_LIMIT),
        name="mix_out",
    )(u, u, u, ya, x, p['w_pool'], p['pool_scale'], p['w_out'], p['g_ffn'])


def _ffn_kernel(hp_ref, hm_ref, hn_ref, x1_ref, wup_ref, cw_ref, cb_ref,
                wdn_ref, gfin_ref, y_ref, hext_ref, ua_ref, ub_ref, acc_ref):
    tile = hm_ref.shape[1]
    i = pl.program_id(1)
    n_tiles = pl.num_programs(1)
    H16 = HALO_BF16
    hext_ref[0:H16, :] = jnp.where(i > 0, hp_ref[0], jnp.zeros_like(hp_ref[0]))
    hext_ref[H16:H16 + tile, :] = hm_ref[0]
    hext_ref[H16 + tile:, :] = jnp.where(i < n_tiles - 1, hn_ref[0],
                                         jnp.zeros_like(hn_ref[0]))
    acc_ref[...] = jnp.zeros_like(acc_ref)

    def up_proj(c, u_ref):
        u_ref[...] = _dot(hext_ref[...], wup_ref[c])

    def conv_down(c, u_ref):
        cw = cw_ref[c]
        conv = (cb_ref[c]
                + u_ref[H16 - 1:H16 - 1 + tile, :] * cw[0:1, :]
                + u_ref[H16:H16 + tile, :] * cw[1:2, :]
                + u_ref[H16 + 1:H16 + 1 + tile, :] * cw[2:3, :])
        gate = conv[:, :F_CHUNK]
        val = conv[:, F_CHUNK:]
        act = (gate * jax.nn.sigmoid(gate) * val).astype(BF16)
        acc_ref[...] += _dot(act, wdn_ref[c])

    up_proj(0, ua_ref)

    def chunk_pair(k, carry):
        c = 2 * k
        up_proj(c + 1, ub_ref)
        conv_down(c, ua_ref)
        up_proj(c + 2, ua_ref)
        conv_down(c + 1, ub_ref)
        return carry

    assert N_F_CHUNKS % 2 == 1
    lax.fori_loop(0, N_F_CHUNKS // 2, chunk_pair, 0)
    conv_down(N_F_CHUNKS - 1, ua_ref)
    y_ref[0] = _rms(x1_ref[0] + acc_ref[...], gfin_ref[...])


def _ffn_call(h2, x1, p, tile):
    B, S, _ = x1.shape
    grid = (B, S // tile)
    tb = tile // HALO_BF16
    nb = S // HALO_BF16
    const2 = lambda b, i: (0, 0)
    const3 = lambda b, i: (0, 0, 0)
    resident = dict(pipeline_mode=pl.Buffered(1))
    return pl.pallas_call(
        _ffn_kernel,
        grid=grid,
        in_specs=[
            pl.BlockSpec((1, HALO_BF16, D_MODEL),
                         lambda b, i: (b, jnp.maximum(i * tb - 1, 0), 0)),
            pl.BlockSpec((1, tile, D_MODEL), lambda b, i: (b, i, 0)),
            pl.BlockSpec((1, HALO_BF16, D_MODEL),
                         lambda b, i: (b, jnp.minimum((i + 1) * tb, nb - 1), 0)),
            pl.BlockSpec((1, tile, D_MODEL), lambda b, i: (b, i, 0)),
            pl.BlockSpec(p['w_up'].shape, const3, **resident),
            pl.BlockSpec(p['conv_w'].shape, const3, **resident),
            pl.BlockSpec(p['conv_b'].shape, const3, **resident),
            pl.BlockSpec(p['w_down'].shape, const3, **resident),
            pl.BlockSpec((1, D_MODEL), const2),
        ],
        out_specs=pl.BlockSpec((1, tile, D_MODEL), lambda b, i: (b, i, 0)),
        out_shape=jax.ShapeDtypeStruct((B, S, D_MODEL), F32),
        scratch_shapes=[
            pltpu.VMEM((tile + 2 * HALO_BF16, D_MODEL), BF16),
            pltpu.VMEM((tile + 2 * HALO_BF16, 2 * F_CHUNK), F32),
            pltpu.VMEM((tile + 2 * HALO_BF16, 2 * F_CHUNK), F32),
            pltpu.VMEM((tile, D_MODEL), F32),
        ],
        compiler_params=pltpu.CompilerParams(
            dimension_semantics=("parallel", "parallel"),
            vmem_limit_bytes=FFN_VMEM_LIMIT),
        name="conv_ffn",
    )(h2, h2, h2, x1, p['w_up'], p['conv_w'], p['conv_b'], p['w_down'], p['g_fin'])


def _prep_params(norm_mix_g, w_in, q_norm_g, w_uq, kv_norm_g, w_ukv, w_pool,
                 pool_scale, w_out, norm_ffn_g, w_up, conv_w, conv_b, w_down,
                 final_norm_g):
    R = HALF_ROPE
    nope = QK_NOPE_DIM
    zeros = lambda r, c: jnp.zeros((r, c), F32)

    o = POOL_WIDTH + Q_LORA_RANK + KV_LORA_RANK
    kx1 = w_in[:, o:o + R]
    kx2 = w_in[:, o + R:o + 2 * R]
    w_in_p = jnp.concatenate(
        [w_in[:, :o], zeros(D_MODEL, nope), kx1, kx2, kx2, kx1], axis=1)

    qa, qb, wkv = [], [], []
    zq = zeros(Q_LORA_RANK, LANES - nope - 2 * R)
    for h in range(N_HEADS):
        c0 = h * QK_HEAD_DIM
        qn_ = w_uq[:, c0:c0 + nope]
        qx1 = w_uq[:, c0 + nope:c0 + nope + R]
        qx2 = w_uq[:, c0 + nope + R:c0 + nope + 2 * R]
        k0 = h * (nope + V_HEAD_DIM)
        kcols = w_ukv[:, k0:k0 + nope]
        vcols = w_ukv[:, k0 + nope:k0 + nope + V_HEAD_DIM]
        if h % 2 == 0:
            qa += [qx1, qx2, zq, qn_]
            qb += [qx2, qx1, zq, zeros(Q_LORA_RANK, nope)]
            wkv += [vcols, kcols]
        else:
            qa += [qn_, qx1, qx2, zq]
            qb += [zeros(Q_LORA_RANK, nope), qx2, qx1, zq]
            wkv += [kcols, vcols]

    nc = N_F_CHUNKS
    gate_w = w_up[:, :D_FF].reshape(D_MODEL, nc, F_CHUNK)
    val_w = w_up[:, D_FF:].reshape(D_MODEL, nc, F_CHUNK)
    w_up_c = jnp.concatenate([gate_w, val_w], axis=2).transpose(1, 0, 2)
    cw_c = jnp.concatenate([conv_w[:, :D_FF].reshape(3, nc, F_CHUNK),
                            conv_w[:, D_FF:].reshape(3, nc, F_CHUNK)],
                           axis=2).transpose(1, 0, 2)
    cb_c = jnp.concatenate([conv_b[:D_FF].reshape(nc, 1, F_CHUNK),
                            conv_b[D_FF:].reshape(nc, 1, F_CHUNK)], axis=2)
    return dict(
        g_mix=norm_mix_g[None, :], w_in=w_in_p.astype(BF16),
        g_q=q_norm_g[None, :], g_kv=kv_norm_g[None, :],
        w_qa=jnp.concatenate(qa, axis=1).astype(BF16),
        w_qb=jnp.concatenate(qb, axis=1).astype(BF16),
        w_kv=jnp.concatenate(wkv, axis=1).astype(BF16),
        w_pool=w_pool.astype(BF16), pool_scale=pool_scale[None, :],
        w_out=w_out.astype(BF16), g_ffn=norm_ffn_g[None, :],
        w_up=w_up_c.astype(BF16), conv_w=cw_c, conv_b=cb_c,
        w_down=w_down.reshape(nc, F_CHUNK, D_MODEL).astype(BF16),
        g_fin=final_norm_g[None, :],
    )


def _rope_tables(seq):
    inv_freq = ROPE_THETA ** (-jnp.arange(0, QK_ROPE_DIM, 2, dtype=F32) / QK_ROPE_DIM)
    ang = jnp.arange(seq, dtype=F32)[:, None] * inv_freq[None, :]
    c, s = jnp.cos(ang), jnp.sin(ang)
    one = lambda n: jnp.ones((seq, n), F32)
    zero = lambda n: jnp.zeros((seq, n), F32)
    rest = LANES - QK_NOPE_DIM - QK_ROPE_DIM
    return jnp.concatenate([
        c, c, one(rest), one(QK_NOPE_DIM),
        -s, s, zero(rest), zero(QK_NOPE_DIM),
        one(QK_NOPE_DIM), c, c, one(rest),
        zero(QK_NOPE_DIM), -s, s, zero(rest)], axis=1)


def _encoder(x, p):
    S = x.shape[1]
    rope = _rope_tables(S)
    u, q, k, v = _pre_call(x, p, rope, tile=512)
    ya = _attn_call(q, k, v, tq=min(S, 2048), q_sub=512, k_chunk=min(2048, S // 2))
    x1, h2 = _mix_call(u, ya, x, p, tile=512)
    return _ffn_call(h2, x1, p, tile=1024)


def kernel(x_prompt, x_sample, norm_mix_g, w_in, q_norm_g, w_uq, kv_norm_g, w_ukv, w_pool, pool_scale, w_out, norm_ffn_g, w_up, conv_w, conv_b, w_down, final_norm_g):
    assert norm_mix_g.shape[0] == 1
    p = _prep_params(norm_mix_g[0], w_in[0], q_norm_g[0], w_uq[0], kv_norm_g[0],
                     w_ukv[0], w_pool[0], pool_scale[0], w_out[0], norm_ffn_g[0],
                     w_up[0], conv_w[0], conv_b[0], w_down[0], final_norm_g)
    return (_encoder(x_prompt, p), _encoder(x_sample, p))
```

```python
import functools
import math

import jax
import jax.numpy as jnp
from jax import lax
from jax.experimental import pallas as pl
from jax.experimental.pallas import tpu as pltpu

D_MODEL = 1024
POOL_WINDOWS = (2, 4, 8, 16)
POOL_WIDTH = 512
POOL_GROUP_DIM = 128
N_HEADS = 8
QK_NOPE_DIM = 64
QK_ROPE_DIM = 32
QK_HEAD_DIM = 96
V_HEAD_DIM = 64
Q_LORA_RANK = 256
KV_LORA_RANK = 128
ROPE_THETA = 10000.0
D_FF = 2816
EPS = 1e-6

LANES = 128
HEAD_PAD = LANES
HALF_ROPE = QK_ROPE_DIM // 2
F_CHUNK = 256
N_F_CHUNKS = D_FF // F_CHUNK
HALO_F32 = 8
HALO_BF16 = 16
VMEM_LIMIT = 48 * 1024 * 1024
FFN_VMEM_LIMIT = 56 * 1024 * 1024

BF16 = jnp.bfloat16
F32 = jnp.float32


def _rms(x, g):
    ms = jnp.mean(x * x, axis=-1, keepdims=True)
    return x * lax.rsqrt(ms + EPS) * g


def _dot(a, b):
    return jnp.dot(a, b, preferred_element_type=F32)


def _pre_kernel(x_ref, gmix_ref, win_ref, gq_ref, gkv_ref, wqa_ref, wqb_ref,
                wkv_ref, rope_ref, u_ref, q_ref, k_ref, v_ref, *, q_scale):
    x = x_ref[0]
    tile = x.shape[0]
    h = _rms(x, gmix_ref[...]).astype(BF16)
    z = _dot(h, win_ref[...])
    u_ref[0] = z[:, :POOL_WIDTH]
    o = POOL_WIDTH
    qn = _rms(z[:, o:o + Q_LORA_RANK], gq_ref[...]).astype(BF16)
    o += Q_LORA_RANK
    kvn = _rms(z[:, o:o + KV_LORA_RANK], gkv_ref[...]).astype(BF16)
    o += KV_LORA_RANK
    cos_e = rope_ref[:, 0:LANES]
    sin_e = rope_ref[:, LANES:2 * LANES]
    cos_o = pltpu.roll(cos_e, LANES // 2, axis=1)
    sin_o = pltpu.roll(sin_e, LANES // 2, axis=1)
    lane = lax.broadcasted_iota(jnp.int32, (tile, LANES), 1)
    low_half = lane < LANES // 2
    kblk = z[:, o:o + LANES]
    kr_o = kblk * cos_o + pltpu.roll(kblk, LANES - QK_ROPE_DIM, axis=1) * sin_o
    kr_o = jnp.where(lane < QK_NOPE_DIM + QK_ROPE_DIM, kr_o, 0.0)
    kr_e = pltpu.roll(kr_o, LANES // 2, axis=1)
    one_e = jnp.where(lane == V_HEAD_DIM, 1.0, 0.0)
    one_o = jnp.where(lane == 0, 1.0, 0.0)
    qa = _dot(qn, wqa_ref[...])
    qb = _dot(qn, wqb_ref[...])
    kv = _dot(kvn, wkv_ref[...])
    for hd in range(N_HEADS):
        sl = slice(hd * HEAD_PAD, (hd + 1) * HEAD_PAD)
        even = hd % 2 == 0
        cos, sin = (cos_e, sin_e) if even else (cos_o, sin_o)
        q_ref[0, hd] = ((qa[:, sl] * cos + qb[:, sl] * sin) * q_scale).astype(BF16)
        blk = kv[:, sl]
        if even:
            k_ref[0, hd] = jnp.where(low_half, kr_e, blk).astype(BF16)
            v_ref[0, hd] = jnp.where(low_half, blk, one_e).astype(BF16)
        else:
            k_ref[0, hd] = jnp.where(low_half, blk, kr_o).astype(BF16)
            v_ref[0, hd] = jnp.where(low_half, one_o, blk).astype(BF16)


def _pre_call(x, p, rope, tile):
    B, S, _ = x.shape
    grid = (B, S // tile)
    const2 = lambda b, i: (0, 0)
    q_scale = math.log2(math.e) / math.sqrt(QK_HEAD_DIM)
    head_shape = jax.ShapeDtypeStruct((B, N_HEADS, S, HEAD_PAD), BF16)
    head_spec = pl.BlockSpec((1, N_HEADS, tile, HEAD_PAD), lambda b, i: (b, 0, i, 0))
    return pl.pallas_call(
        functools.partial(_pre_kernel, q_scale=q_scale),
        grid=grid,
        in_specs=[
            pl.BlockSpec((1, tile, D_MODEL), lambda b, i: (b, i, 0)),
            pl.BlockSpec((1, D_MODEL), const2),
            pl.BlockSpec(p['w_in'].shape, const2),
            pl.BlockSpec((1, Q_LORA_RANK), const2),
            pl.BlockSpec((1, KV_LORA_RANK), const2),
            pl.BlockSpec(p['w_qa'].shape, const2),
            pl.BlockSpec(p['w_qb'].shape, const2),
            pl.BlockSpec(p['w_kv'].shape, const2),
            pl.BlockSpec((tile, 2 * LANES), lambda b, i: (i, 0)),
        ],
        out_specs=[
            pl.BlockSpec((1, tile, POOL_WIDTH), lambda b, i: (b, i, 0)),
            head_spec, head_spec, head_spec,
        ],
        out_shape=[
            jax.ShapeDtypeStruct((B, S, POOL_WIDTH), F32),
            head_shape, head_shape, head_shape,
        ],
        compiler_params=pltpu.CompilerParams(
            dimension_semantics=("parallel", "parallel"),
            vmem_limit_bytes=VMEM_LIMIT),
        name="pre_proj",
    )(x, p['g_mix'], p['w_in'], p['g_q'], p['g_kv'], p['w_qa'], p['w_qb'],
      p['w_kv'], rope)


def _attn_kernel(q_ref, k_ref, v_ref, o_ref, *, q_sub, k_chunk):
    tq = q_ref.shape[2]
    S = k_ref.shape[2]
    n_chunks = S // k_chunk
    lane = lax.broadcasted_iota(jnp.int32, (q_sub, LANES), 1)

    def attend(hd, row0):
        q = q_ref[0, hd, pl.ds(row0, q_sub), :]
        m = jnp.full((q_sub, 1), -1e30, F32)
        acc = jnp.zeros((q_sub, LANES), F32)
        for c in range(n_chunks):
            k = k_ref[0, hd, pl.ds(c * k_chunk, k_chunk), :]
            v = v_ref[0, hd, pl.ds(c * k_chunk, k_chunk), :]
            s = lax.dot_general(q, k, (((1,), (1,)), ((), ())),
                                preferred_element_type=F32)
            m_new = jnp.maximum(m, jnp.max(s, axis=-1, keepdims=True))
            pr = jnp.exp2(s - m_new).astype(BF16)
            pv = _dot(pr, v)
            if c == 0:
                acc = pv
            else:
                acc = acc * jnp.exp2(m - m_new) + pv
            m = m_new
        return acc

    def body(r, carry):
        row0 = pl.multiple_of(r * q_sub, q_sub)
        a0 = attend(0, row0)
        a1 = attend(1, row0)
        l0 = a0[:, V_HEAD_DIM:V_HEAD_DIM + 1]
        l1 = a1[:, 0:1]
        out = jnp.where(lane < V_HEAD_DIM, a0 / l0, a1 / l1)
        o_ref[0, pl.ds(row0, q_sub), :] = out.astype(o_ref.dtype)
        return carry

    lax.fori_loop(0, tq // q_sub, body, 0, unroll=2)


def _attn_call(q, k, v, tq, q_sub, k_chunk):
    B, H, S, _ = q.shape
    grid = (B, H // 2, S // tq)
    return pl.pallas_call(
        functools.partial(_attn_kernel, q_sub=q_sub, k_chunk=k_chunk),
        grid=grid,
        in_specs=[
            pl.BlockSpec((1, 2, tq, HEAD_PAD), lambda b, j, i: (b, j, i, 0)),
            pl.BlockSpec((1, 2, S, HEAD_PAD), lambda b, j, i: (b, j, 0, 0)),
            pl.BlockSpec((1, 2, S, HEAD_PAD), lambda b, j, i: (b, j, 0, 0)),
        ],
        out_specs=pl.BlockSpec((1, tq, LANES), lambda b, j, i: (b, i, j)),
        out_shape=jax.ShapeDtypeStruct((B, S, (H // 2) * LANES), BF16),
        compiler_params=pltpu.CompilerParams(
            dimension_semantics=("parallel", "parallel", "parallel"),
            vmem_limit_bytes=VMEM_LIMIT),
        name="attn",
    )(q, k, v)


def _mix_kernel(up_ref, um_ref, un_ref, ya_ref, x_ref, wpool_ref, pscale_ref,
                wout_ref, gffn_ref, x1_ref, h2_ref, ext_ref, s2_ref, s4_ref, s8_ref,
                *, seq_len):
    tile = um_ref.shape[1]
    i = pl.program_id(1)
    n_tiles = pl.num_programs(1)
    H8 = HALO_F32
    G = POOL_GROUP_DIM
    ext_ref[0:H8, :] = jnp.where(i > 0, up_ref[0], 0.0)
    ext_ref[H8:H8 + tile, :] = um_ref[0]
    ext_ref[H8 + tile:2 * H8 + tile, :] = jnp.where(i < n_tiles - 1, un_ref[0], 0.0)
    ext_ref[2 * H8 + tile:, :] = jnp.zeros((2 * H8, POOL_WIDTH), F32)
    n2, n4, n8 = tile + 3 * H8, tile + 2 * H8, tile + H8
    s2_ref[...] = ext_ref[0:n2, G:] + ext_ref[1:n2 + 1, G:]
    s4_ref[...] = s2_ref[0:n4, :] + s2_ref[2:n4 + 2, :]
    s8_ref[...] = s4_ref[0:n8, G:] + s4_ref[4:n8 + 4, G:]
    window_sums = (
        ext_ref[H8 - 1:H8 - 1 + tile, 0:G] + ext_ref[H8:H8 + tile, 0:G],
        s4_ref[H8 - 2:H8 - 2 + tile, 0:G],
        s8_ref[H8 - 4:H8 - 4 + tile, 0:G],
        s8_ref[0:tile, G:] + s8_ref[H8:H8 + tile, G:],
    )
    pos = i * tile + lax.broadcasted_iota(jnp.int32, (tile, 1), 0)
    parts = []
    for g, w in enumerate(POOL_WINDOWS):
        cs = slice(g * POOL_GROUP_DIM, (g + 1) * POOL_GROUP_DIM)
        half = w // 2
        acc = window_sums[g]
        lo = jnp.maximum(pos - half, 0)
        hi = jnp.minimum(pos + half, seq_len)
        cnt = (hi - lo).astype(F32)
        pooled = acc / cnt - ext_ref[H8:H8 + tile, cs]
        pm = _dot(pooled.astype(BF16), wpool_ref[g]) * pscale_ref[:, cs]
        parts.append(pm.astype(BF16))
    mixed = jnp.concatenate(parts + [ya_ref[0]], axis=-1)
    x1 = x_ref[0] + _dot(mixed, wout_ref[...])
    x1_ref[0] = x1
    h2_ref[0] = _rms(x1, gffn_ref[...]).astype(BF16)


def _mix_call(u, ya, x, p, tile):
    B, S, _ = x.shape
    grid = (B, S // tile)
    tb = tile // HALO_F32
    nb = S // HALO_F32
    const2 = lambda b, i: (0, 0)
    return pl.pallas_call(
        functools.partial(_mix_kernel, seq_len=S),
        grid=grid,
        in_specs=[
            pl.BlockSpec((1, HALO_F32, POOL_WIDTH),
                         lambda b, i: (b, jnp.maximum(i * tb - 1, 0), 0)),
            pl.BlockSpec((1, tile, POOL_WIDTH), lambda b, i: (b, i, 0)),
            pl.BlockSpec((1, HALO_F32, POOL_WIDTH),
                         lambda b, i: (b, jnp.minimum((i + 1) * tb, nb - 1), 0)),
            pl.BlockSpec((1, tile, POOL_WIDTH), lambda b, i: (b, i, 0)),
            pl.BlockSpec((1, tile, D_MODEL), lambda b, i: (b, i, 0)),
            pl.BlockSpec(p['w_pool'].shape, lambda b, i: (0, 0, 0)),
            pl.BlockSpec((1, POOL_WIDTH), const2),
            pl.BlockSpec((D_MODEL, D_MODEL), const2),
            pl.BlockSpec((1, D_MODEL), const2),
        ],
        out_specs=[
            pl.BlockSpec((1, tile, D_MODEL), lambda b, i: (b, i, 0)),
            pl.BlockSpec((1, tile, D_MODEL), lambda b, i: (b, i, 0)),
        ],
        out_shape=[
            jax.ShapeDtypeStruct((B, S, D_MODEL), F32),
            jax.ShapeDtypeStruct((B, S, D_MODEL), BF16),
        ],
        scratch_shapes=[
            pltpu.VMEM((tile + 4 * HALO_F32, POOL_WIDTH), F32),
            pltpu.VMEM((tile + 3 * HALO_F32, 3 * POOL_GROUP_DIM), F32),
            pltpu.VMEM((tile + 2 * HALO_F32, 3 * POOL_GROUP_DIM), F32),
            pltpu.VMEM((tile + HALO_F32, 2 * POOL_GROUP_DIM), F32),
        ],
        compiler_params=pltpu.CompilerParams(
            dimension_semantics=("parallel", "parallel"),
            vmem_limit_bytes=VMEM_LIMIT),
        name="mix_out",
    )(u, u, u, ya, x, p['w_pool'], p['pool_scale'], p['w_out'], p['g_ffn'])


def _ffn_kernel(hp_ref, hm_ref, hn_ref, x1_ref, wup_ref, cw_ref, cb_ref,
                wdn_ref, gfin_ref, y_ref, hext_ref, ua_ref, ub_ref, acc_ref):
    tile = hm_ref.shape[1]
    i = pl.program_id(1)
    n_tiles = pl.num_programs(1)
    H16 = HALO_BF16
    hext_ref[0:H16, :] = jnp.where(i > 0, hp_ref[0], jnp.zeros_like(hp_ref[0]))
    hext_ref[H16:H16 + tile, :] = hm_ref[0]
    hext_ref[H16 + tile:, :] = jnp.where(i < n_tiles - 1, hn_ref[0],
                                         jnp.zeros_like(hn_ref[0]))
    acc_ref[...] = jnp.zeros_like(acc_ref)

    def up_proj(c, u_ref):
        u_ref[...] = _dot(hext_ref[...], wup_ref[c])

    def conv_down(c, u_ref):
        cw = cw_ref[c]
        conv = (cb_ref[c]
                + u_ref[H16 - 1:H16 - 1 + tile, :] * cw[0:1, :]
                + u_ref[H16:H16 + tile, :] * cw[1:2, :]
                + u_ref[H16 + 1:H16 + 1 + tile, :] * cw[2:3, :])
        gate = conv[:, :F_CHUNK]
        val = conv[:, F_CHUNK:]
        act = (gate * jax.nn.sigmoid(gate) * val).astype(BF16)
        acc_ref[...] += _dot(act, wdn_ref[c])

    up_proj(0, ua_ref)

    def chunk_pair(k, carry):
        c = 2 * k
        up_proj(c + 1, ub_ref)
        conv_down(c, ua_ref)
        up_proj(c + 2, ua_ref)
        conv_down(c + 1, ub_ref)
        return carry

    assert N_F_CHUNKS % 2 == 1
    lax.fori_loop(0, N_F_CHUNKS // 2, chunk_pair, 0)
    conv_down(N_F_CHUNKS - 1, ua_ref)
    y_ref[0] = _rms(x1_ref[0] + acc_ref[...], gfin_ref[...])


def _ffn_call(h2, x1, p, tile):
    B, S, _ = x1.shape
    grid = (B, S // tile)
    tb = tile // HALO_BF16
    nb = S // HALO_BF16
    const2 = lambda b, i: (0, 0)
    const3 = lambda b, i: (0, 0, 0)
    resident = dict(pipeline_mode=pl.Buffered(1))
    return pl.pallas_call(
        _ffn_kernel,
        grid=grid,
        in_specs=[
            pl.BlockSpec((1, HALO_BF16, D_MODEL),
                         lambda b, i: (b, jnp.maximum(i * tb - 1, 0), 0)),
            pl.BlockSpec((1, tile, D_MODEL), lambda b, i: (b, i, 0)),
            pl.BlockSpec((1, HALO_BF16, D_MODEL),
                         lambda b, i: (b, jnp.minimum((i + 1) * tb, nb - 1), 0)),
            pl.BlockSpec((1, tile, D_MODEL), lambda b, i: (b, i, 0)),
            pl.BlockSpec(p['w_up'].shape, const3, **resident),
            pl.BlockSpec(p['conv_w'].shape, const3, **resident),
            pl.BlockSpec(p['conv_b'].shape, const3, **resident),
            pl.BlockSpec(p['w_down'].shape, const3, **resident),
            pl.BlockSpec((1, D_MODEL), const2),
        ],
        out_specs=pl.BlockSpec((1, tile, D_MODEL), lambda b, i: (b, i, 0)),
        out_shape=jax.ShapeDtypeStruct((B, S, D_MODEL), F32),
        scratch_shapes=[
            pltpu.VMEM((tile + 2 * HALO_BF16, D_MODEL), BF16),
            pltpu.VMEM((tile + 2 * HALO_BF16, 2 * F_CHUNK), F32),
            pltpu.VMEM((tile + 2 * HALO_BF16, 2 * F_CHUNK), F32),
            pltpu.VMEM((tile, D_MODEL), F32),
        ],
        compiler_params=pltpu.CompilerParams(
            dimension_semantics=("parallel", "parallel"),
            vmem_limit_bytes=FFN_VMEM_LIMIT),
        name="conv_ffn",
    )(h2, h2, h2, x1, p['w_up'], p['conv_w'], p['conv_b'], p['w_down'], p['g_fin'])


def _prep_params(norm_mix_g, w_in, q_norm_g, w_uq, kv_norm_g, w_ukv, w_pool,
                 pool_scale, w_out, norm_ffn_g, w_up, conv_w, conv_b, w_down,
                 final_norm_g):
    R = HALF_ROPE
    nope = QK_NOPE_DIM
    zeros = lambda r, c: jnp.zeros((r, c), F32)

    o = POOL_WIDTH + Q_LORA_RANK + KV_LORA_RANK
    kx1 = w_in[:, o:o + R]
    kx2 = w_in[:, o + R:o + 2 * R]
    w_in_p = jnp.concatenate(
        [w_in[:, :o], zeros(D_MODEL, nope), kx1, kx2, kx2, kx1], axis=1)

    qa, qb, wkv = [], [], []
    zq = zeros(Q_LORA_RANK, LANES - nope - 2 * R)
    for h in range(N_HEADS):
        c0 = h * QK_HEAD_DIM
        qn_ = w_uq[:, c0:c0 + nope]
        qx1 = w_uq[:, c0 + nope:c0 + nope + R]
        qx2 = w_uq[:, c0 + nope + R:c0 + nope + 2 * R]
        k0 = h * (nope + V_HEAD_DIM)
        kcols = w_ukv[:, k0:k0 + nope]
        vcols = w_ukv[:, k0 + nope:k0 + nope + V_HEAD_DIM]
        if h % 2 == 0:
            qa += [qx1, qx2, zq, qn_]
            qb += [qx2, qx1, zq, zeros(Q_LORA_RANK, nope)]
            wkv += [vcols, kcols]
        else:
            qa += [qn_, qx1, qx2, zq]
            qb += [zeros(Q_LORA_RANK, nope), qx2, qx1, zq]
            wkv += [kcols, vcols]

    nc = N_F_CHUNKS
    w_up_b = w_up.astype(BF16)
    gate_w = w_up_b[:, :D_FF].reshape(D_MODEL, nc, F_CHUNK)
    val_w = w_up_b[:, D_FF:].reshape(D_MODEL, nc, F_CHUNK)
    w_up_c = jnp.concatenate([gate_w, val_w], axis=2).transpose(1, 0, 2)
    cw_c = jnp.concatenate([conv_w[:, :D_FF].reshape(3, nc, F_CHUNK),
                            conv_w[:, D_FF:].reshape(3, nc, F_CHUNK)],
                           axis=2).transpose(1, 0, 2)
    cb_c = jnp.concatenate([conv_b[:D_FF].reshape(nc, 1, F_CHUNK),
                            conv_b[D_FF:].reshape(nc, 1, F_CHUNK)], axis=2)
    return dict(
        g_mix=norm_mix_g[None, :], w_in=w_in_p.astype(BF16),
        g_q=q_norm_g[None, :], g_kv=kv_norm_g[None, :],
        w_qa=jnp.concatenate(qa, axis=1).astype(BF16),
        w_qb=jnp.concatenate(qb, axis=1).astype(BF16),
        w_kv=jnp.concatenate(wkv, axis=1).astype(BF16),
        w_pool=w_pool.astype(BF16), pool_scale=pool_scale[None, :],
        w_out=w_out.astype(BF16), g_ffn=norm_ffn_g[None, :],
        w_up=w_up_c, conv_w=cw_c, conv_b=cb_c,
        w_down=w_down.reshape(nc, F_CHUNK, D_MODEL).astype(BF16),
        g_fin=final_norm_g[None, :],
    )


def _rope_tables(seq):
    inv_freq = ROPE_THETA ** (-jnp.arange(0, QK_ROPE_DIM, 2, dtype=F32) / QK_ROPE_DIM)
    ang = jnp.arange(seq, dtype=F32)[:, None] * inv_freq[None, :]
    c, s = jnp.cos(ang), jnp.sin(ang)
    pad = ((0, 0), (0, LANES - QK_ROPE_DIM))
    cos_e = jnp.pad(jnp.concatenate([c, c], axis=1), pad, constant_values=1.0)
    sin_e = jnp.pad(jnp.concatenate([-s, s], axis=1), pad)
    return jnp.concatenate([cos_e, sin_e], axis=1)


def _encoder(x, p):
    S = x.shape[1]
    rope = _rope_tables(S)
    u, q, k, v = _pre_call(x, p, rope, tile=1024)
    ya = _attn_call(q, k, v, tq=min(S, 2048), q_sub=1024, k_chunk=min(2048, S // 2))
    x1, h2 = _mix_call(u, ya, x, p, tile=1024)
    return _ffn_call(h2, x1, p, tile=1024)


def kernel(x_prompt, x_sample, norm_mix_g, w_in, q_norm_g, w_uq, kv_norm_g, w_ukv, w_pool, pool_scale, w_out, norm_ffn_g, w_up, conv_w, conv_b, w_down, final_norm_g):
    assert norm_mix_g.shape[0] == 1
    p = _prep_params(norm_mix_g[0], w_in[0], q_norm_g[0], w_uq[0], kv_norm_g[0],
                     w_ukv[0], w_pool[0], pool_scale[0], w_out[0], norm_ffn_g[0],
                     w_up[0], conv_w[0], conv_b[0], w_down[0], final_norm_g)
    return (_encoder(x_prompt, p), _encoder(x_sample, p))
```

```python
import functools
import math

import jax
import jax.numpy as jnp
from jax import lax
from jax.experimental import pallas as pl
from jax.experimental.pallas import tpu as pltpu

D_MODEL = 1024
POOL_WINDOWS = (2, 4, 8, 16)
POOL_WIDTH = 512
POOL_GROUP_DIM = 128
N_HEADS = 8
QK_NOPE_DIM = 64
QK_ROPE_DIM = 32
QK_HEAD_DIM = 96
V_HEAD_DIM = 64
Q_LORA_RANK = 256
KV_LORA_RANK = 128
ROPE_THETA = 10000.0
D_FF = 2816
EPS = 1e-6

LANES = 128
HEAD_PAD = LANES
HALF_ROPE = QK_ROPE_DIM // 2
F_CHUNK = 256
N_F_CHUNKS = D_FF // F_CHUNK
HALO_F32 = 8
HALO_BF16 = 16
VMEM_LIMIT = 48 * 1024 * 1024
FFN_VMEM_LIMIT = 56 * 1024 * 1024

BF16 = jnp.bfloat16
F32 = jnp.float32


def _rms(x, g):
    ms = jnp.mean(x * x, axis=-1, keepdims=True)
    return x * lax.rsqrt(ms + EPS) * g


def _dot(a, b):
    return jnp.dot(a, b, preferred_element_type=F32)


def _pre_kernel(x_ref, gmix_ref, win_ref, gq_ref, gkv_ref, wqa_ref, wqb_ref,
                wkv_ref, rope_ref, u_ref, q_ref, k_ref, v_ref, *, q_scale):
    x = x_ref[0]
    tile = x.shape[0]
    h = _rms(x, gmix_ref[...]).astype(BF16)
    z = _dot(h, win_ref[...])
    u_ref[0] = z[:, :POOL_WIDTH]
    o = POOL_WIDTH
    qn = _rms(z[:, o:o + Q_LORA_RANK], gq_ref[...]).astype(BF16)
    o += Q_LORA_RANK
    kvn = _rms(z[:, o:o + KV_LORA_RANK], gkv_ref[...]).astype(BF16)
    o += KV_LORA_RANK
    cos_e = rope_ref[:, 0:LANES]
    sin_e = rope_ref[:, LANES:2 * LANES]
    cos_o = pltpu.roll(cos_e, LANES // 2, axis=1)
    sin_o = pltpu.roll(sin_e, LANES // 2, axis=1)
    lane = lax.broadcasted_iota(jnp.int32, (tile, LANES), 1)
    low_half = lane < LANES // 2
    kblk = z[:, o:o + LANES]
    kr_o = kblk * cos_o + pltpu.roll(kblk, LANES - QK_ROPE_DIM, axis=1) * sin_o
    kr_o = jnp.where(lane < QK_NOPE_DIM + QK_ROPE_DIM, kr_o, 0.0)
    kr_e = pltpu.roll(kr_o, LANES // 2, axis=1)
    one_e = jnp.where(lane == V_HEAD_DIM, 1.0, 0.0)
    one_o = jnp.where(lane == 0, 1.0, 0.0)
    qa = _dot(qn, wqa_ref[...])
    qb = _dot(qn, wqb_ref[...])
    kv = _dot(kvn, wkv_ref[...])
    for hd in range(N_HEADS):
        sl = slice(hd * HEAD_PAD, (hd + 1) * HEAD_PAD)
        even = hd % 2 == 0
        cos, sin = (cos_e, sin_e) if even else (cos_o, sin_o)
        q_ref[0, hd] = ((qa[:, sl] * cos + qb[:, sl] * sin) * q_scale).astype(BF16)
        blk = kv[:, sl]
        if even:
            k_ref[0, hd] = jnp.where(low_half, kr_e, blk).astype(BF16)
            v_ref[0, hd] = jnp.where(low_half, blk, one_e).astype(BF16)
        else:
            k_ref[0, hd] = jnp.where(low_half, blk, kr_o).astype(BF16)
            v_ref[0, hd] = jnp.where(low_half, one_o, blk).astype(BF16)


def _pre_call(x, p, rope, tile):
    B, S, _ = x.shape
    grid = (B, S // tile)
    const2 = lambda b, i: (0, 0)
    q_scale = math.log2(math.e) / math.sqrt(QK_HEAD_DIM)
    head_shape = jax.ShapeDtypeStruct((B, N_HEADS, S, HEAD_PAD), BF16)
    head_spec = pl.BlockSpec((1, N_HEADS, tile, HEAD_PAD), lambda b, i: (b, 0, i, 0))
    return pl.pallas_call(
        functools.partial(_pre_kernel, q_scale=q_scale),
        grid=grid,
        in_specs=[
            pl.BlockSpec((1, tile, D_MODEL), lambda b, i: (b, i, 0)),
            pl.BlockSpec((1, D_MODEL), const2),
            pl.BlockSpec(p['w_in'].shape, const2),
            pl.BlockSpec((1, Q_LORA_RANK), const2),
            pl.BlockSpec((1, KV_LORA_RANK), const2),
            pl.BlockSpec(p['w_qa'].shape, const2),
            pl.BlockSpec(p['w_qb'].shape, const2),
            pl.BlockSpec(p['w_kv'].shape, const2),
            pl.BlockSpec((tile, 2 * LANES), lambda b, i: (i, 0)),
        ],
        out_specs=[
            pl.BlockSpec((1, tile, POOL_WIDTH), lambda b, i: (b, i, 0)),
            head_spec, head_spec, head_spec,
        ],
        out_shape=[
            jax.ShapeDtypeStruct((B, S, POOL_WIDTH), F32),
            head_shape, head_shape, head_shape,
        ],
        compiler_params=pltpu.CompilerParams(
            dimension_semantics=("parallel", "parallel"),
            vmem_limit_bytes=VMEM_LIMIT),
        name="pre_proj",
    )(x, p['g_mix'], p['w_in'], p['g_q'], p['g_kv'], p['w_qa'], p['w_qb'],
      p['w_kv'], rope)


def _attn_kernel(q_ref, k_ref, v_ref, o_ref, *, q_sub, k_chunk):
    tq = q_ref.shape[2]
    S = k_ref.shape[2]
    n_chunks = S // k_chunk
    lane = lax.broadcasted_iota(jnp.int32, (q_sub, LANES), 1)

    def attend(hd, row0):
        q = q_ref[0, hd, pl.ds(row0, q_sub), :]
        m = jnp.full((q_sub, 1), -1e30, F32)
        acc = jnp.zeros((q_sub, LANES), F32)
        for c in range(n_chunks):
            k = k_ref[0, hd, pl.ds(c * k_chunk, k_chunk), :]
            v = v_ref[0, hd, pl.ds(c * k_chunk, k_chunk), :]
            s = lax.dot_general(q, k, (((1,), (1,)), ((), ())),
                                preferred_element_type=F32)
            m_new = jnp.maximum(m, jnp.max(s, axis=-1, keepdims=True))
            pr = jnp.exp2(s - m_new).astype(BF16)
            pv = _dot(pr, v)
            if c == 0:
                acc = pv
            else:
                acc = acc * jnp.exp2(m - m_new) + pv
            m = m_new
        return acc

    def body(r, carry):
        row0 = pl.multiple_of(r * q_sub, q_sub)
        a0 = attend(0, row0)
        a1 = attend(1, row0)
        l0 = a0[:, V_HEAD_DIM:V_HEAD_DIM + 1]
        l1 = a1[:, 0:1]
        out = jnp.where(lane < V_HEAD_DIM, a0 / l0, a1 / l1)
        o_ref[0, pl.ds(row0, q_sub), :] = out.astype(o_ref.dtype)
        return carry

    lax.fori_loop(0, tq // q_sub, body, 0, unroll=2)


def _attn_call(q, k, v, tq, q_sub, k_chunk):
    B, H, S, _ = q.shape
    grid = (B, H // 2, S // tq)
    return pl.pallas_call(
        functools.partial(_attn_kernel, q_sub=q_sub, k_chunk=k_chunk),
        grid=grid,
        in_specs=[
            pl.BlockSpec((1, 2, tq, HEAD_PAD), lambda b, j, i: (b, j, i, 0)),
            pl.BlockSpec((1, 2, S, HEAD_PAD), lambda b, j, i: (b, j, 0, 0)),
            pl.BlockSpec((1, 2, S, HEAD_PAD), lambda b, j, i: (b, j, 0, 0)),
        ],
        out_specs=pl.BlockSpec((1, tq, LANES), lambda b, j, i: (b, i, j)),
        out_shape=jax.ShapeDtypeStruct((B, S, (H // 2) * LANES), BF16),
        compiler_params=pltpu.CompilerParams(
            dimension_semantics=("parallel", "parallel", "parallel"),
            vmem_limit_bytes=VMEM_LIMIT),
        name="attn",
    )(q, k, v)


def _mix_kernel(up_ref, um_ref, un_ref, ya_ref, x_ref, wpool_ref, pscale_ref,
                wout_ref, gffn_ref, x1_ref, h2_ref, ext_ref, s2_ref, s4_ref, s8_ref,
                *, seq_len):
    tile = um_ref.shape[1]
    i = pl.program_id(1)
    n_tiles = pl.num_programs(1)
    H8 = HALO_F32
    G = POOL_GROUP_DIM
    ext_ref[0:H8, :] = jnp.where(i > 0, up_ref[0], 0.0)
    ext_ref[H8:H8 + tile, :] = um_ref[0]
    ext_ref[H8 + tile:2 * H8 + tile, :] = jnp.where(i < n_tiles - 1, un_ref[0], 0.0)
    ext_ref[2 * H8 + tile:, :] = jnp.zeros((2 * H8, POOL_WIDTH), F32)
    n2, n4, n8 = tile + 3 * H8, tile + 2 * H8, tile + H8
    s2_ref[...] = ext_ref[0:n2, G:] + ext_ref[1:n2 + 1, G:]
    s4_ref[...] = s2_ref[0:n4, :] + s2_ref[2:n4 + 2, :]
    s8_ref[...] = s4_ref[0:n8, G:] + s4_ref[4:n8 + 4, G:]
    window_sums = (
        ext_ref[H8 - 1:H8 - 1 + tile, 0:G] + ext_ref[H8:H8 + tile, 0:G],
        s4_ref[H8 - 2:H8 - 2 + tile, 0:G],
        s8_ref[H8 - 4:H8 - 4 + tile, 0:G],
        s8_ref[0:tile, G:] + s8_ref[H8:H8 + tile, G:],
    )
    pos = i * tile + lax.broadcasted_iota(jnp.int32, (tile, 1), 0)
    parts = []
    for g, w in enumerate(POOL_WINDOWS):
        cs = slice(g * POOL_GROUP_DIM, (g + 1) * POOL_GROUP_DIM)
        half = w // 2
        acc = window_sums[g]
        lo = jnp.maximum(pos - half, 0)
        hi = jnp.minimum(pos + half, seq_len)
        cnt = (hi - lo).astype(F32)
        pooled = acc / cnt - ext_ref[H8:H8 + tile, cs]
        pm = _dot(pooled.astype(BF16), wpool_ref[g]) * pscale_ref[:, cs]
        parts.append(pm.astype(BF16))
    mixed = jnp.concatenate(parts + [ya_ref[0]], axis=-1)
    x1 = x_ref[0] + _dot(mixed, wout_ref[...])
    x1_ref[0] = x1
    h2_ref[0] = _rms(x1, gffn_ref[...]).astype(BF16)


def _mix_call(u, ya, x, p, tile):
    B, S, _ = x.shape
    grid = (B, S // tile)
    tb = tile // HALO_F32
    nb = S // HALO_F32
    const2 = lambda b, i: (0, 0)
    return pl.pallas_call(
        functools.partial(_mix_kernel, seq_len=S),
        grid=grid,
        in_specs=[
            pl.BlockSpec((1, HALO_F32, POOL_WIDTH),
                         lambda b, i: (b, jnp.maximum(i * tb - 1, 0), 0)),
            pl.BlockSpec((1, tile, POOL_WIDTH), lambda b, i: (b, i, 0)),
            pl.BlockSpec((1, HALO_F32, POOL_WIDTH),
                         lambda b, i: (b, jnp.minimum((i + 1) * tb, nb - 1), 0)),
            pl.BlockSpec((1, tile, POOL_WIDTH), lambda b, i: (b, i, 0)),
            pl.BlockSpec((1, tile, D_MODEL), lambda b, i: (b, i, 0)),
            pl.BlockSpec(p['w_pool'].shape, lambda b, i: (0, 0, 0)),
            pl.BlockSpec((1, POOL_WIDTH), const2),
            pl.BlockSpec((D_MODEL, D_MODEL), const2),
            pl.BlockSpec((1, D_MODEL), const2),
        ],
        out_specs=[
            pl.BlockSpec((1, tile, D_MODEL), lambda b, i: (b, i, 0)),
            pl.BlockSpec((1, tile, D_MODEL), lambda b, i: (b, i, 0)),
        ],
        out_shape=[
            jax.ShapeDtypeStruct((B, S, D_MODEL), F32),
            jax.ShapeDtypeStruct((B, S, D_MODEL), BF16),
        ],
        scratch_shapes=[
            pltpu.VMEM((tile + 4 * HALO_F32, POOL_WIDTH), F32),
            pltpu.VMEM((tile + 3 * HALO_F32, 3 * POOL_GROUP_DIM), F32),
            pltpu.VMEM((tile + 2 * HALO_F32, 3 * POOL_GROUP_DIM), F32),
            pltpu.VMEM((tile + HALO_F32, 2 * POOL_GROUP_DIM), F32),
        ],
        compiler_params=pltpu.CompilerParams(
            dimension_semantics=("parallel", "parallel"),
            vmem_limit_bytes=VMEM_LIMIT),
        name="mix_out",
    )(u, u, u, ya, x, p['w_pool'], p['pool_scale'], p['w_out'], p['g_ffn'])


def _ffn_kernel(hp_ref, hm_ref, hn_ref, x1_ref, wup_ref, cw_ref, cb_ref,
                wdn_ref, gfin_ref, y_ref, hext_ref, ua_ref, ub_ref, acc_ref):
    tile = hm_ref.shape[1]
    i = pl.program_id(1)
    n_tiles = pl.num_programs(1)
    H16 = HALO_BF16
    hext_ref[0:H16, :] = jnp.where(i > 0, hp_ref[0], jnp.zeros_like(hp_ref[0]))
    hext_ref[H16:H16 + tile, :] = hm_ref[0]
    hext_ref[H16 + tile:, :] = jnp.where(i < n_tiles - 1, hn_ref[0],
                                         jnp.zeros_like(hn_ref[0]))
    acc_ref[...] = jnp.zeros_like(acc_ref)

    def up_proj(c, u_ref):
        u_ref[...] = _dot(hext_ref[...], wup_ref[c])

    def conv_down(c, u_ref):
        cw = cw_ref[c]
        conv = (cb_ref[c]
                + u_ref[H16 - 1:H16 - 1 + tile, :] * cw[0:1, :]
                + u_ref[H16:H16 + tile, :] * cw[1:2, :]
                + u_ref[H16 + 1:H16 + 1 + tile, :] * cw[2:3, :])
        gate = conv[:, :F_CHUNK]
        val = conv[:, F_CHUNK:]
        act = (gate * jax.nn.sigmoid(gate) * val).astype(BF16)
        acc_ref[...] += _dot(act, wdn_ref[c])

    up_proj(0, ua_ref)

    def chunk_pair(k, carry):
        c = 2 * k
        up_proj(c + 1, ub_ref)
        conv_down(c, ua_ref)
        up_proj(c + 2, ua_ref)
        conv_down(c + 1, ub_ref)
        return carry

    assert N_F_CHUNKS % 2 == 1
    lax.fori_loop(0, N_F_CHUNKS // 2, chunk_pair, 0)
    conv_down(N_F_CHUNKS - 1, ua_ref)
    y_ref[0] = _rms(x1_ref[0] + acc_ref[...], gfin_ref[...])


def _ffn_call(h2, x1, p, tile):
    B, S, _ = x1.shape
    grid = (B, S // tile)
    tb = tile // HALO_BF16
    nb = S // HALO_BF16
    const2 = lambda b, i: (0, 0)
    const3 = lambda b, i: (0, 0, 0)
    resident = dict(pipeline_mode=pl.Buffered(1))
    return pl.pallas_call(
        _ffn_kernel,
        grid=grid,
        in_specs=[
            pl.BlockSpec((1, HALO_BF16, D_MODEL),
                         lambda b, i: (b, jnp.maximum(i * tb - 1, 0), 0)),
            pl.BlockSpec((1, tile, D_MODEL), lambda b, i: (b, i, 0)),
            pl.BlockSpec((1, HALO_BF16, D_MODEL),
                         lambda b, i: (b, jnp.minimum((i + 1) * tb, nb - 1), 0)),
            pl.BlockSpec((1, tile, D_MODEL), lambda b, i: (b, i, 0)),
            pl.BlockSpec(p['w_up'].shape, const3, **resident),
            pl.BlockSpec(p['conv_w'].shape, const3, **resident),
            pl.BlockSpec(p['conv_b'].shape, const3, **resident),
            pl.BlockSpec(p['w_down'].shape, const3, **resident),
            pl.BlockSpec((1, D_MODEL), const2),
        ],
        out_specs=pl.BlockSpec((1, tile, D_MODEL), lambda b, i: (b, i, 0)),
        out_shape=jax.ShapeDtypeStruct((B, S, D_MODEL), F32),
        scratch_shapes=[
            pltpu.VMEM((tile + 2 * HALO_BF16, D_MODEL), BF16),
            pltpu.VMEM((tile + 2 * HALO_BF16, 2 * F_CHUNK), F32),
            pltpu.VMEM((tile + 2 * HALO_BF16, 2 * F_CHUNK), F32),
            pltpu.VMEM((tile, D_MODEL), F32),
        ],
        compiler_params=pltpu.CompilerParams(
            dimension_semantics=("parallel", "parallel"),
            vmem_limit_bytes=FFN_VMEM_LIMIT),
        name="conv_ffn",
    )(h2, h2, h2, x1, p['w_up'], p['conv_w'], p['conv_b'], p['w_down'], p['g_fin'])


def _prep_params(norm_mix_g, w_in, q_norm_g, w_uq, kv_norm_g, w_ukv, w_pool,
                 pool_scale, w_out, norm_ffn_g, w_up, conv_w, conv_b, w_down,
                 final_norm_g):
    R = HALF_ROPE
    nope = QK_NOPE_DIM
    zeros = lambda r, c: jnp.zeros((r, c), F32)

    o = POOL_WIDTH + Q_LORA_RANK + KV_LORA_RANK
    kx1 = w_in[:, o:o + R]
    kx2 = w_in[:, o + R:o + 2 * R]
    w_in_p = jnp.concatenate(
        [w_in[:, :o], zeros(D_MODEL, nope), kx1, kx2, kx2, kx1], axis=1)

    qa, qb, wkv = [], [], []
    zq = zeros(Q_LORA_RANK, LANES - nope - 2 * R)
    for h in range(N_HEADS):
        c0 = h * QK_HEAD_DIM
        qn_ = w_uq[:, c0:c0 + nope]
        qx1 = w_uq[:, c0 + nope:c0 + nope + R]
        qx2 = w_uq[:, c0 + nope + R:c0 + nope + 2 * R]
        k0 = h * (nope + V_HEAD_DIM)
        kcols = w_ukv[:, k0:k0 + nope]
        vcols = w_ukv[:, k0 + nope:k0 + nope + V_HEAD_DIM]
        if h % 2 == 0:
            qa += [qx1, qx2, zq, qn_]
            qb += [qx2, qx1, zq, zeros(Q_LORA_RANK, nope)]
            wkv += [vcols, kcols]
        else:
            qa += [qn_, qx1, qx2, zq]
            qb += [zeros(Q_LORA_RANK, nope), qx2, qx1, zq]
            wkv += [kcols, vcols]

    nc = N_F_CHUNKS
    w_up_b = w_up.astype(BF16)
    gate_w = w_up_b[:, :D_FF].reshape(D_MODEL, nc, F_CHUNK)
    val_w = w_up_b[:, D_FF:].reshape(D_MODEL, nc, F_CHUNK)
    w_up_c = jnp.concatenate([gate_w, val_w], axis=2).transpose(1, 0, 2)
    cw_c = jnp.concatenate([conv_w[:, :D_FF].reshape(3, nc, F_CHUNK),
                            conv_w[:, D_FF:].reshape(3, nc, F_CHUNK)],
                           axis=2).transpose(1, 0, 2)
    cb_c = jnp.concatenate([conv_b[:D_FF].reshape(nc, 1, F_CHUNK),
                            conv_b[D_FF:].reshape(nc, 1, F_CHUNK)], axis=2)
    return dict(
        g_mix=norm_mix_g[None, :], w_in=w_in_p.astype(BF16),
        g_q=q_norm_g[None, :], g_kv=kv_norm_g[None, :],
        w_qa=jnp.concatenate(qa, axis=1).astype(BF16),
        w_qb=jnp.concatenate(qb, axis=1).astype(BF16),
        w_kv=jnp.concatenate(wkv, axis=1).astype(BF16),
        w_pool=w_pool.astype(BF16), pool_scale=pool_scale[None, :],
        w_out=w_out.astype(BF16), g_ffn=norm_ffn_g[None, :],
        w_up=w_up_c, conv_w=cw_c, conv_b=cb_c,
        w_down=w_down.reshape(nc, F_CHUNK, D_MODEL).astype(BF16),
        g_fin=final_norm_g[None, :],
    )


def _rope_tables(seq):
    inv_freq = ROPE_THETA ** (-jnp.arange(0, QK_ROPE_DIM, 2, dtype=F32) / QK_ROPE_DIM)
    ang = jnp.arange(seq, dtype=F32)[:, None] * inv_freq[None, :]
    c, s = jnp.cos(ang), jnp.sin(ang)
    pad = ((0, 0), (0, LANES - QK_ROPE_DIM))
    cos_e = jnp.pad(jnp.concatenate([c, c], axis=1), pad, constant_values=1.0)
    sin_e = jnp.pad(jnp.concatenate([-s, s], axis=1), pad)
    return jnp.concatenate([cos_e, sin_e], axis=1)


def _encoder(x, p):
    S = x.shape[1]
    rope = _rope_tables(S)
    u, q, k, v = _pre_call(x, p, rope, tile=1024)
    k_chunk = min(2048, S // 2)
    q_sub = (1 << 20) // k_chunk
    ya = _attn_call(q, k, v, tq=min(S, 2048), q_sub=q_sub, k_chunk=k_chunk)
    x1, h2 = _mix_call(u, ya, x, p, tile=1024)
    return _ffn_call(h2, x1, p, tile=1024)


def kernel(x_prompt, x_sample, norm_mix_g, w_in, q_norm_g, w_uq, kv_norm_g, w_ukv, w_pool, pool_scale, w_out, norm_ffn_g, w_up, conv_w, conv_b, w_down, final_norm_g):
    assert norm_mix_g.shape[0] == 1
    p = _prep_params(norm_mix_g[0], w_in[0], q_norm_g[0], w_uq[0], kv_norm_g[0],
                     w_ukv[0], w_pool[0], pool_scale[0], w_out[0], norm_ffn_g[0],
                     w_up[0], conv_w[0], conv_b[0], w_down[0], final_norm_g)
    return (_encoder(x_prompt, p), _encoder(x_sample, p))
```

```python
import functools
import math

import jax
import jax.numpy as jnp
from jax import lax
from jax.experimental import pallas as pl
from jax.experimental.pallas import tpu as pltpu

D_MODEL = 1024
POOL_WINDOWS = (2, 4, 8, 16)
POOL_WIDTH = 512
POOL_GROUP_DIM = 128
N_HEADS = 8
QK_NOPE_DIM = 64
QK_ROPE_DIM = 32
QK_HEAD_DIM = 96
V_HEAD_DIM = 64
Q_LORA_RANK = 256
KV_LORA_RANK = 128
ROPE_THETA = 10000.0
D_FF = 2816
EPS = 1e-6

LANES = 128
HEAD_PAD = LANES
HALF_ROPE = QK_ROPE_DIM // 2
F_CHUNK = 256
assert D_FF % F_CHUNK == 0
N_F_CHUNKS = D_FF // F_CHUNK
HALO_F32 = 8
HALO_BF16 = 16
MIB = 1024 * 1024
VMEM_LIMIT = 48 * MIB
FFN_VMEM_LIMIT = 56 * MIB
ROW_TILE = 1024
ATTN_Q_TILE = 2048
ATTN_MAX_KEY_CHUNK = 2048
SCORE_TILE_BYTES = 4 * MIB
RUNNING_MAX_INIT = -1e30
BF16 = jnp.bfloat16
F32 = jnp.float32


def _rms(x, g):
    ms = jnp.mean(x * x, axis=-1, keepdims=True)
    return x * lax.rsqrt(ms + EPS) * g


def _dot(a, b):
    return jnp.dot(a, b, preferred_element_type=F32)


def _pre_kernel(x_ref, gmix_ref, win_ref, gq_ref, gkv_ref, wqa_ref, wqb_ref,
                wkv_ref, rope_ref, u_ref, q_ref, k_ref, v_ref, *, q_scale):
    x = x_ref[0]
    tile = x.shape[0]
    h = _rms(x, gmix_ref[...]).astype(BF16)
    z = _dot(h, win_ref[...])
    u_ref[0] = z[:, :POOL_WIDTH]
    o = POOL_WIDTH
    qn = _rms(z[:, o:o + Q_LORA_RANK], gq_ref[...]).astype(BF16)
    o += Q_LORA_RANK
    kvn = _rms(z[:, o:o + KV_LORA_RANK], gkv_ref[...]).astype(BF16)
    o += KV_LORA_RANK
    cos_e = rope_ref[:, 0:LANES]
    sin_e = rope_ref[:, LANES:2 * LANES]
    cos_o = pltpu.roll(cos_e, LANES // 2, axis=1)
    sin_o = pltpu.roll(sin_e, LANES // 2, axis=1)
    lane = lax.broadcasted_iota(jnp.int32, (tile, LANES), 1)
    low_half = lane < LANES // 2
    kblk = z[:, o:o + LANES]
    kr_o = kblk * cos_o + pltpu.roll(kblk, LANES - QK_ROPE_DIM, axis=1) * sin_o
    kr_o = jnp.where(lane < QK_NOPE_DIM + QK_ROPE_DIM, kr_o, 0.0)
    kr_e = pltpu.roll(kr_o, LANES // 2, axis=1)
    one_e = jnp.where(lane == V_HEAD_DIM, 1.0, 0.0)
    one_o = jnp.where(lane == 0, 1.0, 0.0)
    qa = _dot(qn, wqa_ref[...])
    qb = _dot(qn, wqb_ref[...])
    kv = _dot(kvn, wkv_ref[...])
    for hd in range(N_HEADS):
        sl = slice(hd * HEAD_PAD, (hd + 1) * HEAD_PAD)
        even = hd % 2 == 0
        cos, sin = (cos_e, sin_e) if even else (cos_o, sin_o)
        q_ref[0, hd] = ((qa[:, sl] * cos + qb[:, sl] * sin) * q_scale).astype(BF16)
        blk = kv[:, sl]
        if even:
            k_ref[0, hd] = jnp.where(low_half, kr_e, blk).astype(BF16)
            v_ref[0, hd] = jnp.where(low_half, blk, one_e).astype(BF16)
        else:
            k_ref[0, hd] = jnp.where(low_half, blk, kr_o).astype(BF16)
            v_ref[0, hd] = jnp.where(low_half, one_o, blk).astype(BF16)


def _pre_call(x, p, rope, tile):
    B, S, _ = x.shape
    grid = (B, S // tile)
    const2 = lambda b, i: (0, 0)
    q_scale = math.log2(math.e) / math.sqrt(QK_HEAD_DIM)
    head_shape = jax.ShapeDtypeStruct((B, N_HEADS, S, HEAD_PAD), BF16)
    head_spec = pl.BlockSpec((1, N_HEADS, tile, HEAD_PAD), lambda b, i: (b, 0, i, 0))
    return pl.pallas_call(
        functools.partial(_pre_kernel, q_scale=q_scale),
        grid=grid,
        in_specs=[
            pl.BlockSpec((1, tile, D_MODEL), lambda b, i: (b, i, 0)),
            pl.BlockSpec((1, D_MODEL), const2),
            pl.BlockSpec(p['w_in'].shape, const2),
            pl.BlockSpec((1, Q_LORA_RANK), const2),
            pl.BlockSpec((1, KV_LORA_RANK), const2),
            pl.BlockSpec(p['w_qa'].shape, const2),
            pl.BlockSpec(p['w_qb'].shape, const2),
            pl.BlockSpec(p['w_kv'].shape, const2),
            pl.BlockSpec((tile, 2 * LANES), lambda b, i: (i, 0)),
        ],
        out_specs=[
            pl.BlockSpec((1, tile, POOL_WIDTH), lambda b, i: (b, i, 0)),
            head_spec, head_spec, head_spec,
        ],
        out_shape=[
            jax.ShapeDtypeStruct((B, S, POOL_WIDTH), F32),
            head_shape, head_shape, head_shape,
        ],
        compiler_params=pltpu.CompilerParams(
            dimension_semantics=("parallel", "parallel"),
            vmem_limit_bytes=VMEM_LIMIT),
        name="pre_proj",
    )(x, p['g_mix'], p['w_in'], p['g_q'], p['g_kv'], p['w_qa'], p['w_qb'],
      p['w_kv'], rope)


def _attn_kernel(q_ref, k_ref, v_ref, o_ref, *, q_sub, k_chunk):
    tq = q_ref.shape[2]
    S = k_ref.shape[2]
    n_chunks = S // k_chunk
    lane = lax.broadcasted_iota(jnp.int32, (q_sub, LANES), 1)

    def attend(hd, row0):
        q = q_ref[0, hd, pl.ds(row0, q_sub), :]
        m = jnp.full((q_sub, 1), RUNNING_MAX_INIT, F32)
        acc = jnp.zeros((q_sub, LANES), F32)
        for c in range(n_chunks):
            k = k_ref[0, hd, pl.ds(c * k_chunk, k_chunk), :]
            v = v_ref[0, hd, pl.ds(c * k_chunk, k_chunk), :]
            s = lax.dot_general(q, k, (((1,), (1,)), ((), ())),
                                preferred_element_type=F32)
            m_new = jnp.maximum(m, jnp.max(s, axis=-1, keepdims=True))
            pr = jnp.exp2(s - m_new).astype(BF16)
            pv = _dot(pr, v)
            if c == 0:
                acc = pv
            else:
                acc = acc * jnp.exp2(m - m_new) + pv
            m = m_new
        return acc

    def body(r, carry):
        row0 = pl.multiple_of(r * q_sub, q_sub)
        a0 = attend(0, row0)
        a1 = attend(1, row0)
        l0 = a0[:, V_HEAD_DIM:V_HEAD_DIM + 1]
        l1 = a1[:, 0:1]
        out = jnp.where(lane < V_HEAD_DIM, a0 / l0, a1 / l1)
        o_ref[0, pl.ds(row0, q_sub), :] = out.astype(o_ref.dtype)
        return carry

    lax.fori_loop(0, tq // q_sub, body, 0, unroll=2)


def _attn_call(q, k, v, tq, q_sub, k_chunk):
    B, H, S, _ = q.shape
    grid = (B, H // 2, S // tq)
    return pl.pallas_call(
        functools.partial(_attn_kernel, q_sub=q_sub, k_chunk=k_chunk),
        grid=grid,
        in_specs=[
            pl.BlockSpec((1, 2, tq, HEAD_PAD), lambda b, j, i: (b, j, i, 0)),
            pl.BlockSpec((1, 2, S, HEAD_PAD), lambda b, j, i: (b, j, 0, 0)),
            pl.BlockSpec((1, 2, S, HEAD_PAD), lambda b, j, i: (b, j, 0, 0)),
        ],
        out_specs=pl.BlockSpec((1, tq, LANES), lambda b, j, i: (b, i, j)),
        out_shape=jax.ShapeDtypeStruct((B, S, (H // 2) * LANES), BF16),
        compiler_params=pltpu.CompilerParams(
            dimension_semantics=("parallel", "parallel", "parallel"),
            vmem_limit_bytes=VMEM_LIMIT),
        name="attn",
    )(q, k, v)


def _mix_kernel(up_ref, um_ref, un_ref, ya_ref, x_ref, wpool_ref, pscale_ref,
                wout_ref, gffn_ref, x1_ref, h2_ref, ext_ref, s2_ref, s4_ref, s8_ref,
                *, seq_len):
    tile = um_ref.shape[1]
    i = pl.program_id(1)
    n_tiles = pl.num_programs(1)
    H8 = HALO_F32
    G = POOL_GROUP_DIM
    ext_ref[0:H8, :] = jnp.where(i > 0, up_ref[0], 0.0)
    ext_ref[H8:H8 + tile, :] = um_ref[0]
    ext_ref[H8 + tile:2 * H8 + tile, :] = jnp.where(i < n_tiles - 1, un_ref[0], 0.0)
    ext_ref[2 * H8 + tile:, :] = jnp.zeros((2 * H8, POOL_WIDTH), F32)
    n2, n4, n8 = tile + 3 * H8, tile + 2 * H8, tile + H8
    s2_ref[...] = ext_ref[0:n2, G:] + ext_ref[1:n2 + 1, G:]
    s4_ref[...] = s2_ref[0:n4, :] + s2_ref[2:n4 + 2, :]
    s8_ref[...] = s4_ref[0:n8, G:] + s4_ref[4:n8 + 4, G:]
    window_sums = (
        ext_ref[H8 - 1:H8 - 1 + tile, 0:G] + ext_ref[H8:H8 + tile, 0:G],
        s4_ref[H8 - 2:H8 - 2 + tile, 0:G],
        s8_ref[H8 - 4:H8 - 4 + tile, 0:G],
        s8_ref[0:tile, G:] + s8_ref[H8:H8 + tile, G:],
    )
    pos = i * tile + lax.broadcasted_iota(jnp.int32, (tile, 1), 0)
    parts = []
    for g, w in enumerate(POOL_WINDOWS):
        cs = slice(g * G, (g + 1) * G)
        half = w // 2
        lo = jnp.maximum(pos - half, 0)
        hi = jnp.minimum(pos + half, seq_len)
        cnt = (hi - lo).astype(F32)
        pooled = window_sums[g] / cnt - ext_ref[H8:H8 + tile, cs]
        pm = _dot(pooled.astype(BF16), wpool_ref[g]) * pscale_ref[:, cs]
        parts.append(pm.astype(BF16))
    mixed = jnp.concatenate(parts + [ya_ref[0]], axis=-1)
    x1 = x_ref[0] + _dot(mixed, wout_ref[...])
    x1_ref[0] = x1
    h2_ref[0] = _rms(x1, gffn_ref[...]).astype(BF16)


def _mix_call(u, ya, x, p, tile):
    B, S, _ = x.shape
    grid = (B, S // tile)
    tb = tile // HALO_F32
    nb = S // HALO_F32
    const2 = lambda b, i: (0, 0)
    return pl.pallas_call(
        functools.partial(_mix_kernel, seq_len=S),
        grid=grid,
        in_specs=[
            pl.BlockSpec((1, HALO_F32, POOL_WIDTH),
                         lambda b, i: (b, jnp.maximum(i * tb - 1, 0), 0)),
            pl.BlockSpec((1, tile, POOL_WIDTH), lambda b, i: (b, i, 0)),
            pl.BlockSpec((1, HALO_F32, POOL_WIDTH),
                         lambda b, i: (b, jnp.minimum((i + 1) * tb, nb - 1), 0)),
            pl.BlockSpec((1, tile, POOL_WIDTH), lambda b, i: (b, i, 0)),
            pl.BlockSpec((1, tile, D_MODEL), lambda b, i: (b, i, 0)),
            pl.BlockSpec(p['w_pool'].shape, lambda b, i: (0, 0, 0)),
            pl.BlockSpec((1, POOL_WIDTH), const2),
            pl.BlockSpec((D_MODEL, D_MODEL), const2),
            pl.BlockSpec((1, D_MODEL), const2),
        ],
        out_specs=[
            pl.BlockSpec((1, tile, D_MODEL), lambda b, i: (b, i, 0)),
            pl.BlockSpec((1, tile, D_MODEL), lambda b, i: (b, i, 0)),
        ],
        out_shape=[
            jax.ShapeDtypeStruct((B, S, D_MODEL), F32),
            jax.ShapeDtypeStruct((B, S, D_MODEL), BF16),
        ],
        scratch_shapes=[
            pltpu.VMEM((tile + 4 * HALO_F32, POOL_WIDTH), F32),
            pltpu.VMEM((tile + 3 * HALO_F32, 3 * POOL_GROUP_DIM), F32),
            pltpu.VMEM((tile + 2 * HALO_F32, 3 * POOL_GROUP_DIM), F32),
            pltpu.VMEM((tile + HALO_F32, 2 * POOL_GROUP_DIM), F32),
        ],
        compiler_params=pltpu.CompilerParams(
            dimension_semantics=("parallel", "parallel"),
            vmem_limit_bytes=VMEM_LIMIT),
        name="mix_out",
    )(u, u, u, ya, x, p['w_pool'], p['pool_scale'], p['w_out'], p['g_ffn'])


def _ffn_kernel(hp_ref, hm_ref, hn_ref, x1_ref, wup_ref, cw_ref, cb_ref,
                wdn_ref, gfin_ref, y_ref, hext_ref, ua_ref, ub_ref, acc_ref):
    tile = hm_ref.shape[1]
    i = pl.program_id(1)
    n_tiles = pl.num_programs(1)
    H16 = HALO_BF16
    hext_ref[0:H16, :] = jnp.where(i > 0, hp_ref[0], jnp.zeros_like(hp_ref[0]))
    hext_ref[H16:H16 + tile, :] = hm_ref[0]
    hext_ref[H16 + tile:, :] = jnp.where(i < n_tiles - 1, hn_ref[0],
                                         jnp.zeros_like(hn_ref[0]))
    acc_ref[...] = jnp.zeros_like(acc_ref)

    def up_proj(c, u_ref):
        u_ref[...] = _dot(hext_ref[...], wup_ref[c])

    def conv_down(c, u_ref):
        cw = cw_ref[c]
        conv = (cb_ref[c]
                + u_ref[H16 - 1:H16 - 1 + tile, :] * cw[0:1, :]
                + u_ref[H16:H16 + tile, :] * cw[1:2, :]
                + u_ref[H16 + 1:H16 + 1 + tile, :] * cw[2:3, :])
        gate = conv[:, :F_CHUNK]
        val = conv[:, F_CHUNK:]
        act = (gate * jax.nn.sigmoid(gate) * val).astype(BF16)
        acc_ref[...] += _dot(act, wdn_ref[c])

    up_proj(0, ua_ref)

    def chunk_pair(k, carry):
        c = 2 * k
        up_proj(c + 1, ub_ref)
        conv_down(c, ua_ref)
        up_proj(c + 2, ua_ref)
        conv_down(c + 1, ub_ref)
        return carry

    assert N_F_CHUNKS % 2 == 1
    lax.fori_loop(0, N_F_CHUNKS // 2, chunk_pair, 0)
    conv_down(N_F_CHUNKS - 1, ua_ref)
    y_ref[0] = _rms(x1_ref[0] + acc_ref[...], gfin_ref[...])


def _ffn_call(h2, x1, p, tile):
    B, S, _ = x1.shape
    grid = (B, S // tile)
    tb = tile // HALO_BF16
    nb = S // HALO_BF16
    const2 = lambda b, i: (0, 0)
    const3 = lambda b, i: (0, 0, 0)
    resident = dict(pipeline_mode=pl.Buffered(1))
    return pl.pallas_call(
        _ffn_kernel,
        grid=grid,
        in_specs=[
            pl.BlockSpec((1, HALO_BF16, D_MODEL),
                         lambda b, i: (b, jnp.maximum(i * tb - 1, 0), 0)),
            pl.BlockSpec((1, tile, D_MODEL), lambda b, i: (b, i, 0)),
            pl.BlockSpec((1, HALO_BF16, D_MODEL),
                         lambda b, i: (b, jnp.minimum((i + 1) * tb, nb - 1), 0)),
            pl.BlockSpec((1, tile, D_MODEL), lambda b, i: (b, i, 0)),
            pl.BlockSpec(p['w_up'].shape, const3, **resident),
            pl.BlockSpec(p['conv_w'].shape, const3, **resident),
            pl.BlockSpec(p['conv_b'].shape, const3, **resident),
            pl.BlockSpec(p['w_down'].shape, const3, **resident),
            pl.BlockSpec((1, D_MODEL), const2),
        ],
        out_specs=pl.BlockSpec((1, tile, D_MODEL), lambda b, i: (b, i, 0)),
        out_shape=jax.ShapeDtypeStruct((B, S, D_MODEL), F32),
        scratch_shapes=[
            pltpu.VMEM((tile + 2 * HALO_BF16, D_MODEL), BF16),
            pltpu.VMEM((tile + 2 * HALO_BF16, 2 * F_CHUNK), F32),
            pltpu.VMEM((tile + 2 * HALO_BF16, 2 * F_CHUNK), F32),
            pltpu.VMEM((tile, D_MODEL), F32),
        ],
        compiler_params=pltpu.CompilerParams(
            dimension_semantics=("parallel", "parallel"),
            vmem_limit_bytes=FFN_VMEM_LIMIT),
        name="conv_ffn",
    )(h2, h2, h2, x1, p['w_up'], p['conv_w'], p['conv_b'], p['w_down'], p['g_fin'])


def _prep_params(norm_mix_g, w_in, q_norm_g, w_uq, kv_norm_g, w_ukv, w_pool,
                 pool_scale, w_out, norm_ffn_g, w_up, conv_w, conv_b, w_down,
                 final_norm_g):
    R = HALF_ROPE
    nope = QK_NOPE_DIM
    zeros = lambda r, c: jnp.zeros((r, c), F32)

    o = POOL_WIDTH + Q_LORA_RANK + KV_LORA_RANK
    kx1 = w_in[:, o:o + R]
    kx2 = w_in[:, o + R:o + 2 * R]
    w_in_p = jnp.concatenate(
        [w_in[:, :o], zeros(D_MODEL, nope), kx1, kx2, kx2, kx1], axis=1)

    qa, qb, wkv = [], [], []
    zq = zeros(Q_LORA_RANK, LANES - nope - 2 * R)
    for h in range(N_HEADS):
        c0 = h * QK_HEAD_DIM
        qn_ = w_uq[:, c0:c0 + nope]
        qx1 = w_uq[:, c0 + nope:c0 + nope + R]
        qx2 = w_uq[:, c0 + nope + R:c0 + nope + 2 * R]
        k0 = h * (nope + V_HEAD_DIM)
        kcols = w_ukv[:, k0:k0 + nope]
        vcols = w_ukv[:, k0 + nope:k0 + nope + V_HEAD_DIM]
        if h % 2 == 0:
            qa += [qx1, qx2, zq, qn_]
            qb += [qx2, qx1, zq, zeros(Q_LORA_RANK, nope)]
            wkv += [vcols, kcols]
        else:
            qa += [qn_, qx1, qx2, zq]
            qb += [zeros(Q_LORA_RANK, nope), qx2, qx1, zq]
            wkv += [kcols, vcols]

    nc = N_F_CHUNKS
    w_up_b = w_up.astype(BF16)
    gate_w = w_up_b[:, :D_FF].reshape(D_MODEL, nc, F_CHUNK)
    val_w = w_up_b[:, D_FF:].reshape(D_MODEL, nc, F_CHUNK)
    w_up_c = jnp.concatenate([gate_w, val_w], axis=2).transpose(1, 0, 2)
    cw_c = jnp.concatenate([conv_w[:, :D_FF].reshape(3, nc, F_CHUNK),
                            conv_w[:, D_FF:].reshape(3, nc, F_CHUNK)],
                           axis=2).transpose(1, 0, 2)
    cb_c = jnp.concatenate([conv_b[:D_FF].reshape(nc, 1, F_CHUNK),
                            conv_b[D_FF:].reshape(nc, 1, F_CHUNK)], axis=2)
    return dict(
        g_mix=norm_mix_g[None, :], w_in=w_in_p.astype(BF16),
        g_q=q_norm_g[None, :], g_kv=kv_norm_g[None, :],
        w_qa=jnp.concatenate(qa, axis=1).astype(BF16),
        w_qb=jnp.concatenate(qb, axis=1).astype(BF16),
        w_kv=jnp.concatenate(wkv, axis=1).astype(BF16),
        w_pool=w_pool.astype(BF16), pool_scale=pool_scale[None, :],
        w_out=w_out.astype(BF16), g_ffn=norm_ffn_g[None, :],
        w_up=w_up_c, conv_w=cw_c, conv_b=cb_c,
        w_down=w_down.reshape(nc, F_CHUNK, D_MODEL).astype(BF16),
        g_fin=final_norm_g[None, :],
    )


def _rope_tables(seq):
    inv_freq = ROPE_THETA ** (-jnp.arange(0, QK_ROPE_DIM, 2, dtype=F32) / QK_ROPE_DIM)
    ang = jnp.arange(seq, dtype=F32)[:, None] * inv_freq[None, :]
    c, s = jnp.cos(ang), jnp.sin(ang)
    pad = ((0, 0), (0, LANES - QK_ROPE_DIM))
    cos_e = jnp.pad(jnp.concatenate([c, c], axis=1), pad, constant_values=1.0)
    sin_e = jnp.pad(jnp.concatenate([-s, s], axis=1), pad)
    return jnp.concatenate([cos_e, sin_e], axis=1)


def _tile_plan(seq):
    row_tile = min(ROW_TILE, seq)
    tq = min(ATTN_Q_TILE, seq)
    k_chunk = min(ATTN_MAX_KEY_CHUNK, seq // 2)
    q_sub = min(tq, SCORE_TILE_BYTES // 4 // k_chunk)
    assert seq % row_tile == 0 and seq % tq == 0 and seq % k_chunk == 0
    assert tq % (2 * q_sub) == 0
    return row_tile, tq, q_sub, k_chunk


def _encoder(x, p):
    S = x.shape[1]
    row_tile, tq, q_sub, k_chunk = _tile_plan(S)
    u, q, k, v = _pre_call(x, p, _rope_tables(S), tile=row_tile)
    ya = _attn_call(q, k, v, tq=tq, q_sub=q_sub, k_chunk=k_chunk)
    x1, h2 = _mix_call(u, ya, x, p, tile=row_tile)
    return _ffn_call(h2, x1, p, tile=row_tile)


def kernel(x_prompt, x_sample, norm_mix_g, w_in, q_norm_g, w_uq, kv_norm_g, w_ukv, w_pool, pool_scale, w_out, norm_ffn_g, w_up, conv_w, conv_b, w_down, final_norm_g):
    assert norm_mix_g.shape[0] == 1
    p = _prep_params(norm_mix_g[0], w_in[0], q_norm_g[0], w_uq[0], kv_norm_g[0],
                     w_ukv[0], w_pool[0], pool_scale[0], w_out[0], norm_ffn_g[0],
                     w_up[0], conv_w[0], conv_b[0], w_down[0], final_norm_g)
    return (_encoder(x_prompt, p), _encoder(x_sample, p))
```

```python
import functools
import math

import jax
import jax.numpy as jnp
from jax import lax
from jax.experimental import pallas as pl
from jax.experimental.pallas import tpu as pltpu

D_MODEL = 1024
POOL_WINDOWS = (2, 4, 8, 16)
POOL_WIDTH = 512
POOL_GROUP_DIM = 128
N_HEADS = 8
QK_NOPE_DIM = 64
QK_ROPE_DIM = 32
QK_HEAD_DIM = 96
V_HEAD_DIM = 64
Q_LORA_RANK = 256
KV_LORA_RANK = 128
ROPE_THETA = 10000.0
D_FF = 2816
EPS = 1e-6

LANES = 128
HEAD_PAD = LANES
HALF_ROPE = QK_ROPE_DIM // 2
F_CHUNK = 256
assert D_FF % F_CHUNK == 0
N_F_CHUNKS = D_FF // F_CHUNK
HALO_F32 = 8
HALO_BF16 = 16
MIB = 1024 * 1024
VMEM_LIMIT = 48 * MIB
FFN_VMEM_LIMIT = 56 * MIB
ROW_TILE = 1024
ATTN_Q_TILE = 2048
ATTN_MAX_KEY_CHUNK = 2048
SCORE_TILE_BYTES = 4 * MIB
RUNNING_MAX_INIT = -1e30
BF16 = jnp.bfloat16
F32 = jnp.float32


def _rms(x, g):
    ms = jnp.mean(x * x, axis=-1, keepdims=True)
    return x * lax.rsqrt(ms + EPS) * g


def _dot(a, b):
    return jnp.dot(a, b, preferred_element_type=F32)


def _pre_kernel(x_ref, gmix_ref, win_ref, gq_ref, gkv_ref, wqa_ref, wqb_ref,
                wkv_ref, rope_ref, u_ref, q_ref, k_ref, v_ref, *, q_scale):
    x = x_ref[0]
    tile = x.shape[0]
    h = _rms(x, gmix_ref[...]).astype(BF16)
    z = _dot(h, win_ref[...])
    u_ref[0] = z[:, :POOL_WIDTH]
    o = POOL_WIDTH
    qn = _rms(z[:, o:o + Q_LORA_RANK], gq_ref[...]).astype(BF16)
    o += Q_LORA_RANK
    kvn = _rms(z[:, o:o + KV_LORA_RANK], gkv_ref[...]).astype(BF16)
    o += KV_LORA_RANK
    cos_e = rope_ref[:, 0:LANES]
    sin_e = rope_ref[:, LANES:2 * LANES]
    cos_o = pltpu.roll(cos_e, LANES // 2, axis=1)
    sin_o = pltpu.roll(sin_e, LANES // 2, axis=1)
    lane = lax.broadcasted_iota(jnp.int32, (tile, LANES), 1)
    low_half = lane < LANES // 2
    kblk = z[:, o:o + LANES]
    kr_o = kblk * cos_o + pltpu.roll(kblk, LANES - QK_ROPE_DIM, axis=1) * sin_o
    kr_o = jnp.where(lane < QK_NOPE_DIM + QK_ROPE_DIM, kr_o, 0.0)
    kr_e = pltpu.roll(kr_o, LANES // 2, axis=1)
    one_e = jnp.where(lane == V_HEAD_DIM, 1.0, 0.0)
    one_o = jnp.where(lane == 0, 1.0, 0.0)
    qa = _dot(qn, wqa_ref[...])
    qb = _dot(qn, wqb_ref[...])
    kv = _dot(kvn, wkv_ref[...])
    for hd in range(N_HEADS):
        sl = slice(hd * HEAD_PAD, (hd + 1) * HEAD_PAD)
        even = hd % 2 == 0
        cos, sin = (cos_e, sin_e) if even else (cos_o, sin_o)
        q_ref[0, hd] = ((qa[:, sl] * cos + qb[:, sl] * sin) * q_scale).astype(BF16)
        blk = kv[:, sl]
        if even:
            k_ref[0, hd] = jnp.where(low_half, kr_e, blk).astype(BF16)
            v_ref[0, hd] = jnp.where(low_half, blk, one_e).astype(BF16)
        else:
            k_ref[0, hd] = jnp.where(low_half, blk, kr_o).astype(BF16)
            v_ref[0, hd] = jnp.where(low_half, one_o, blk).astype(BF16)


def _pre_call(x, p, rope, tile):
    B, S, _ = x.shape
    grid = (B, S // tile)
    const2 = lambda b, i: (0, 0)
    q_scale = math.log2(math.e) / math.sqrt(QK_HEAD_DIM)
    head_shape = jax.ShapeDtypeStruct((B, N_HEADS, S, HEAD_PAD), BF16)
    head_spec = pl.BlockSpec((1, N_HEADS, tile, HEAD_PAD), lambda b, i: (b, 0, i, 0))
    return pl.pallas_call(
        functools.partial(_pre_kernel, q_scale=q_scale),
        grid=grid,
        in_specs=[
            pl.BlockSpec((1, tile, D_MODEL), lambda b, i: (b, i, 0)),
            pl.BlockSpec((1, D_MODEL), const2),
            pl.BlockSpec(p['w_in'].shape, const2),
            pl.BlockSpec((1, Q_LORA_RANK), const2),
            pl.BlockSpec((1, KV_LORA_RANK), const2),
            pl.BlockSpec(p['w_qa'].shape, const2),
            pl.BlockSpec(p['w_qb'].shape, const2),
            pl.BlockSpec(p['w_kv'].shape, const2),
            pl.BlockSpec((tile, 2 * LANES), lambda b, i: (i, 0)),
        ],
        out_specs=[
            pl.BlockSpec((1, tile, POOL_WIDTH), lambda b, i: (b, i, 0)),
            head_spec, head_spec, head_spec,
        ],
        out_shape=[
            jax.ShapeDtypeStruct((B, S, POOL_WIDTH), F32),
            head_shape, head_shape, head_shape,
        ],
        compiler_params=pltpu.CompilerParams(
            dimension_semantics=("parallel", "parallel"),
            vmem_limit_bytes=VMEM_LIMIT),
        name="pre_proj",
    )(x, p['g_mix'], p['w_in'], p['g_q'], p['g_kv'], p['w_qa'], p['w_qb'],
      p['w_kv'], rope)


def _attn_kernel(q_ref, k_ref, v_ref, o_ref, *, q_sub, k_chunk):
    tq = q_ref.shape[2]
    S = k_ref.shape[2]
    n_chunks = S // k_chunk
    lane = lax.broadcasted_iota(jnp.int32, (q_sub, LANES), 1)

    def attend(hd, row0):
        q = q_ref[0, hd, pl.ds(row0, q_sub), :]
        m = jnp.full((q_sub, 1), RUNNING_MAX_INIT, F32)
        acc = jnp.zeros((q_sub, LANES), F32)
        for c in range(n_chunks):
            k = k_ref[0, hd, pl.ds(c * k_chunk, k_chunk), :]
            v = v_ref[0, hd, pl.ds(c * k_chunk, k_chunk), :]
            s = lax.dot_general(q, k, (((1,), (1,)), ((), ())),
                                preferred_element_type=F32)
            m_new = jnp.maximum(m, jnp.max(s, axis=-1, keepdims=True))
            pr = jnp.exp2(s - m_new).astype(BF16)
            pv = _dot(pr, v)
            if c == 0:
                acc = pv
            else:
                acc = acc * jnp.exp2(m - m_new) + pv
            m = m_new
        return acc

    def body(r, carry):
        row0 = pl.multiple_of(r * q_sub, q_sub)
        a0 = attend(0, row0)
        a1 = attend(1, row0)
        l0 = a0[:, V_HEAD_DIM:V_HEAD_DIM + 1]
        l1 = a1[:, 0:1]
        out = jnp.where(lane < V_HEAD_DIM, a0 / l0, a1 / l1)
        o_ref[0, pl.ds(row0, q_sub), :] = out.astype(o_ref.dtype)
        return carry

    lax.fori_loop(0, tq // q_sub, body, 0, unroll=True)


def _attn_call(q, k, v, tq, q_sub, k_chunk):
    B, H, S, _ = q.shape
    grid = (B, H // 2, S // tq)
    return pl.pallas_call(
        functools.partial(_attn_kernel, q_sub=q_sub, k_chunk=k_chunk),
        grid=grid,
        in_specs=[
            pl.BlockSpec((1, 2, tq, HEAD_PAD), lambda b, j, i: (b, j, i, 0)),
            pl.BlockSpec((1, 2, S, HEAD_PAD), lambda b, j, i: (b, j, 0, 0)),
            pl.BlockSpec((1, 2, S, HEAD_PAD), lambda b, j, i: (b, j, 0, 0)),
        ],
        out_specs=pl.BlockSpec((1, tq, LANES), lambda b, j, i: (b, i, j)),
        out_shape=jax.ShapeDtypeStruct((B, S, (H // 2) * LANES), BF16),
        compiler_params=pltpu.CompilerParams(
            dimension_semantics=("parallel", "parallel", "parallel"),
            vmem_limit_bytes=VMEM_LIMIT),
        name="attn",
    )(q, k, v)


def _mix_kernel(up_ref, um_ref, un_ref, ya_ref, x_ref, wpool_ref, pscale_ref,
                wout_ref, gffn_ref, x1_ref, h2_ref, ext_ref, s2_ref, s4_ref, s8_ref,
                *, seq_len):
    tile = um_ref.shape[1]
    i = pl.program_id(1)
    n_tiles = pl.num_programs(1)
    H8 = HALO_F32
    G = POOL_GROUP_DIM
    ext_ref[0:H8, :] = jnp.where(i > 0, up_ref[0], 0.0)
    ext_ref[H8:H8 + tile, :] = um_ref[0]
    ext_ref[H8 + tile:2 * H8 + tile, :] = jnp.where(i < n_tiles - 1, un_ref[0], 0.0)
    ext_ref[2 * H8 + tile:, :] = jnp.zeros((2 * H8, POOL_WIDTH), F32)
    n2, n4, n8 = tile + 3 * H8, tile + 2 * H8, tile + H8
    s2_ref[...] = ext_ref[0:n2, G:] + ext_ref[1:n2 + 1, G:]
    s4_ref[...] = s2_ref[0:n4, :] + s2_ref[2:n4 + 2, :]
    s8_ref[...] = s4_ref[0:n8, G:] + s4_ref[4:n8 + 4, G:]
    window_sums = (
        ext_ref[H8 - 1:H8 - 1 + tile, 0:G] + ext_ref[H8:H8 + tile, 0:G],
        s4_ref[H8 - 2:H8 - 2 + tile, 0:G],
        s8_ref[H8 - 4:H8 - 4 + tile, 0:G],
        s8_ref[0:tile, G:] + s8_ref[H8:H8 + tile, G:],
    )
    pos = i * tile + lax.broadcasted_iota(jnp.int32, (tile, 1), 0)
    parts = []
    for g, w in enumerate(POOL_WINDOWS):
        cs = slice(g * G, (g + 1) * G)
        half = w // 2
        lo = jnp.maximum(pos - half, 0)
        hi = jnp.minimum(pos + half, seq_len)
        cnt = (hi - lo).astype(F32)
        pooled = window_sums[g] / cnt - ext_ref[H8:H8 + tile, cs]
        pm = _dot(pooled.astype(BF16), wpool_ref[g]) * pscale_ref[:, cs]
        parts.append(pm.astype(BF16))
    mixed = jnp.concatenate(parts + [ya_ref[0]], axis=-1)
    x1 = x_ref[0] + _dot(mixed, wout_ref[...])
    x1_ref[0] = x1
    h2_ref[0] = _rms(x1, gffn_ref[...]).astype(BF16)


def _mix_call(u, ya, x, p, tile):
    B, S, _ = x.shape
    grid = (B, S // tile)
    tb = tile // HALO_F32
    nb = S // HALO_F32
    const2 = lambda b, i: (0, 0)
    return pl.pallas_call(
        functools.partial(_mix_kernel, seq_len=S),
        grid=grid,
        in_specs=[
            pl.BlockSpec((1, HALO_F32, POOL_WIDTH),
                         lambda b, i: (b, jnp.maximum(i * tb - 1, 0), 0)),
            pl.BlockSpec((1, tile, POOL_WIDTH), lambda b, i: (b, i, 0)),
            pl.BlockSpec((1, HALO_F32, POOL_WIDTH),
                         lambda b, i: (b, jnp.minimum((i + 1) * tb, nb - 1), 0)),
            pl.BlockSpec((1, tile, POOL_WIDTH), lambda b, i: (b, i, 0)),
            pl.BlockSpec((1, tile, D_MODEL), lambda b, i: (b, i, 0)),
            pl.BlockSpec(p['w_pool'].shape, lambda b, i: (0, 0, 0)),
            pl.BlockSpec((1, POOL_WIDTH), const2),
            pl.BlockSpec((D_MODEL, D_MODEL), const2),
            pl.BlockSpec((1, D_MODEL), const2),
        ],
        out_specs=[
            pl.BlockSpec((1, tile, D_MODEL), lambda b, i: (b, i, 0)),
            pl.BlockSpec((1, tile, D_MODEL), lambda b, i: (b, i, 0)),
        ],
        out_shape=[
            jax.ShapeDtypeStruct((B, S, D_MODEL), F32),
            jax.ShapeDtypeStruct((B, S, D_MODEL), BF16),
        ],
        scratch_shapes=[
            pltpu.VMEM((tile + 4 * HALO_F32, POOL_WIDTH), F32),
            pltpu.VMEM((tile + 3 * HALO_F32, 3 * POOL_GROUP_DIM), F32),
            pltpu.VMEM((tile + 2 * HALO_F32, 3 * POOL_GROUP_DIM), F32),
            pltpu.VMEM((tile + HALO_F32, 2 * POOL_GROUP_DIM), F32),
        ],
        compiler_params=pltpu.CompilerParams(
            dimension_semantics=("parallel", "parallel"),
            vmem_limit_bytes=VMEM_LIMIT),
        name="mix_out",
    )(u, u, u, ya, x, p['w_pool'], p['pool_scale'], p['w_out'], p['g_ffn'])


def _ffn_kernel(hp_ref, hm_ref, hn_ref, x1_ref, wup_ref, cw_ref, cb_ref,
                wdn_ref, gfin_ref, y_ref, hext_ref, ua_ref, ub_ref, acc_ref):
    tile = hm_ref.shape[1]
    i = pl.program_id(1)
    n_tiles = pl.num_programs(1)
    H16 = HALO_BF16
    hext_ref[0:H16, :] = jnp.where(i > 0, hp_ref[0], jnp.zeros_like(hp_ref[0]))
    hext_ref[H16:H16 + tile, :] = hm_ref[0]
    hext_ref[H16 + tile:, :] = jnp.where(i < n_tiles - 1, hn_ref[0],
                                         jnp.zeros_like(hn_ref[0]))
    acc_ref[...] = jnp.zeros_like(acc_ref)

    def up_proj(c, u_ref):
        u_ref[...] = _dot(hext_ref[...], wup_ref[c])

    def conv_down(c, u_ref):
        cw = cw_ref[c]
        conv = (cb_ref[c]
                + u_ref[H16 - 1:H16 - 1 + tile, :] * cw[0:1, :]
                + u_ref[H16:H16 + tile, :] * cw[1:2, :]
                + u_ref[H16 + 1:H16 + 1 + tile, :] * cw[2:3, :])
        gate = conv[:, :F_CHUNK]
        val = conv[:, F_CHUNK:]
        act = (gate * jax.nn.sigmoid(gate) * val).astype(BF16)
        acc_ref[...] += _dot(act, wdn_ref[c])

    up_proj(0, ua_ref)

    def chunk_pair(k, carry):
        c = 2 * k
        up_proj(c + 1, ub_ref)
        conv_down(c, ua_ref)
        up_proj(c + 2, ua_ref)
        conv_down(c + 1, ub_ref)
        return carry

    assert N_F_CHUNKS % 2 == 1
    lax.fori_loop(0, N_F_CHUNKS // 2, chunk_pair, 0)
    conv_down(N_F_CHUNKS - 1, ua_ref)
    y_ref[0] = _rms(x1_ref[0] + acc_ref[...], gfin_ref[...])


def _ffn_call(h2, x1, p, tile):
    B, S, _ = x1.shape
    grid = (B, S // tile)
    tb = tile // HALO_BF16
    nb = S // HALO_BF16
    const2 = lambda b, i: (0, 0)
    const3 = lambda b, i: (0, 0, 0)
    resident = dict(pipeline_mode=pl.Buffered(1))
    return pl.pallas_call(
        _ffn_kernel,
        grid=grid,
        in_specs=[
            pl.BlockSpec((1, HALO_BF16, D_MODEL),
                         lambda b, i: (b, jnp.maximum(i * tb - 1, 0), 0)),
            pl.BlockSpec((1, tile, D_MODEL), lambda b, i: (b, i, 0)),
            pl.BlockSpec((1, HALO_BF16, D_MODEL),
                         lambda b, i: (b, jnp.minimum((i + 1) * tb, nb - 1), 0)),
            pl.BlockSpec((1, tile, D_MODEL), lambda b, i: (b, i, 0)),
            pl.BlockSpec(p['w_up'].shape, const3, **resident),
            pl.BlockSpec(p['conv_w'].shape, const3, **resident),
            pl.BlockSpec(p['conv_b'].shape, const3, **resident),
            pl.BlockSpec(p['w_down'].shape, const3, **resident),
            pl.BlockSpec((1, D_MODEL), const2),
        ],
        out_specs=pl.BlockSpec((1, tile, D_MODEL), lambda b, i: (b, i, 0)),
        out_shape=jax.ShapeDtypeStruct((B, S, D_MODEL), F32),
        scratch_shapes=[
            pltpu.VMEM((tile + 2 * HALO_BF16, D_MODEL), BF16),
            pltpu.VMEM((tile + 2 * HALO_BF16, 2 * F_CHUNK), F32),
            pltpu.VMEM((tile + 2 * HALO_BF16, 2 * F_CHUNK), F32),
            pltpu.VMEM((tile, D_MODEL), F32),
        ],
        compiler_params=pltpu.CompilerParams(
            dimension_semantics=("parallel", "parallel"),
            vmem_limit_bytes=FFN_VMEM_LIMIT),
        name="conv_ffn",
    )(h2, h2, h2, x1, p['w_up'], p['conv_w'], p['conv_b'], p['w_down'], p['g_fin'])


def _prep_params(norm_mix_g, w_in, q_norm_g, w_uq, kv_norm_g, w_ukv, w_pool,
                 pool_scale, w_out, norm_ffn_g, w_up, conv_w, conv_b, w_down,
                 final_norm_g):
    R = HALF_ROPE
    nope = QK_NOPE_DIM
    zeros = lambda r, c: jnp.zeros((r, c), F32)

    o = POOL_WIDTH + Q_LORA_RANK + KV_LORA_RANK
    kx1 = w_in[:, o:o + R]
    kx2 = w_in[:, o + R:o + 2 * R]
    w_in_p = jnp.concatenate(
        [w_in[:, :o], zeros(D_MODEL, nope), kx1, kx2, kx2, kx1], axis=1)

    qa, qb, wkv = [], [], []
    zq = zeros(Q_LORA_RANK, LANES - nope - 2 * R)
    for h in range(N_HEADS):
        c0 = h * QK_HEAD_DIM
        qn_ = w_uq[:, c0:c0 + nope]
        qx1 = w_uq[:, c0 + nope:c0 + nope + R]
        qx2 = w_uq[:, c0 + nope + R:c0 + nope + 2 * R]
        k0 = h * (nope + V_HEAD_DIM)
        kcols = w_ukv[:, k0:k0 + nope]
        vcols = w_ukv[:, k0 + nope:k0 + nope + V_HEAD_DIM]
        if h % 2 == 0:
            qa += [qx1, qx2, zq, qn_]
            qb += [qx2, qx1, zq, zeros(Q_LORA_RANK, nope)]
            wkv += [vcols, kcols]
        else:
            qa += [qn_, qx1, qx2, zq]
            qb += [zeros(Q_LORA_RANK, nope), qx2, qx1, zq]
            wkv += [kcols, vcols]

    nc = N_F_CHUNKS
    w_up_b = w_up.astype(BF16)
    gate_w = w_up_b[:, :D_FF].reshape(D_MODEL, nc, F_CHUNK)
    val_w = w_up_b[:, D_FF:].reshape(D_MODEL, nc, F_CHUNK)
    w_up_c = jnp.concatenate([gate_w, val_w], axis=2).transpose(1, 0, 2)
    cw_c = jnp.concatenate([conv_w[:, :D_FF].reshape(3, nc, F_CHUNK),
                            conv_w[:, D_FF:].reshape(3, nc, F_CHUNK)],
                           axis=2).transpose(1, 0, 2)
    cb_c = jnp.concatenate([conv_b[:D_FF].reshape(nc, 1, F_CHUNK),
                            conv_b[D_FF:].reshape(nc, 1, F_CHUNK)], axis=2)
    return dict(
        g_mix=norm_mix_g[None, :], w_in=w_in_p.astype(BF16),
        g_q=q_norm_g[None, :], g_kv=kv_norm_g[None, :],
        w_qa=jnp.concatenate(qa, axis=1).astype(BF16),
        w_qb=jnp.concatenate(qb, axis=1).astype(BF16),
        w_kv=jnp.concatenate(wkv, axis=1).astype(BF16),
        w_pool=w_pool.astype(BF16), pool_scale=pool_scale[None, :],
        w_out=w_out.astype(BF16), g_ffn=norm_ffn_g[None, :],
        w_up=w_up_c, conv_w=cw_c, conv_b=cb_c,
        w_down=w_down.reshape(nc, F_CHUNK, D_MODEL).astype(BF16),
        g_fin=final_norm_g[None, :],
    )


def _rope_tables(seq):
    inv_freq = ROPE_THETA ** (-jnp.arange(0, QK_ROPE_DIM, 2, dtype=F32) / QK_ROPE_DIM)
    ang = jnp.arange(seq, dtype=F32)[:, None] * inv_freq[None, :]
    c, s = jnp.cos(ang), jnp.sin(ang)
    pad = ((0, 0), (0, LANES - QK_ROPE_DIM))
    cos_e = jnp.pad(jnp.concatenate([c, c], axis=1), pad, constant_values=1.0)
    sin_e = jnp.pad(jnp.concatenate([-s, s], axis=1), pad)
    return jnp.concatenate([cos_e, sin_e], axis=1)


def _tile_plan(seq):
    row_tile = min(ROW_TILE, seq)
    tq = min(ATTN_Q_TILE, seq)
    k_chunk = min(ATTN_MAX_KEY_CHUNK, seq // 2)
    q_sub = min(tq, SCORE_TILE_BYTES // 4 // k_chunk)
    assert seq % row_tile == 0 and seq % tq == 0 and seq % k_chunk == 0
    assert tq % q_sub == 0
    return row_tile, tq, q_sub, k_chunk


def _encoder(x, p):
    S = x.shape[1]
    row_tile, tq, q_sub, k_chunk = _tile_plan(S)
    u, q, k, v = _pre_call(x, p, _rope_tables(S), tile=row_tile)
    ya = _attn_call(q, k, v, tq=tq, q_sub=q_sub, k_chunk=k_chunk)
    x1, h2 = _mix_call(u, ya, x, p, tile=row_tile)
    return _ffn_call(h2, x1, p, tile=row_tile)


def kernel(x_prompt, x_sample, norm_mix_g, w_in, q_norm_g, w_uq, kv_norm_g, w_ukv, w_pool, pool_scale, w_out, norm_ffn_g, w_up, conv_w, conv_b, w_down, final_norm_g):
    assert norm_mix_g.shape[0] == 1
    p = _prep_params(norm_mix_g[0], w_in[0], q_norm_g[0], w_uq[0], kv_norm_g[0],
                     w_ukv[0], w_pool[0], pool_scale[0], w_out[0], norm_ffn_g[0],
                     w_up[0], conv_w[0], conv_b[0], w_down[0], final_norm_g)
    return (_encoder(x_prompt, p), _encoder(x_sample, p))
```

```python
import functools
import math

import jax
import jax.numpy as jnp
from jax import lax
from jax.experimental import pallas as pl
from jax.experimental.pallas import tpu as pltpu

D_MODEL = 1024
POOL_WINDOWS = (2, 4, 8, 16)
POOL_WIDTH = 512
POOL_GROUP_DIM = 128
N_HEADS = 8
QK_NOPE_DIM = 64
QK_ROPE_DIM = 32
QK_HEAD_DIM = 96
V_HEAD_DIM = 64
Q_LORA_RANK = 256
KV_LORA_RANK = 128
ROPE_THETA = 10000.0
D_FF = 2816
EPS = 1e-6

LANES = 128
HEAD_PAD = LANES
HALF_ROPE = QK_ROPE_DIM // 2
F_CHUNK = 256
assert D_FF % F_CHUNK == 0
N_F_CHUNKS = D_FF // F_CHUNK
HALO_F32 = 8
HALO_BF16 = 16
MIB = 1024 * 1024
VMEM_LIMIT = 48 * MIB
FFN_VMEM_LIMIT = 56 * MIB
ROW_TILE = 1024
ATTN_Q_TILE = 2048
ATTN_MAX_KEY_CHUNK = 2048
SCORE_TILE_BYTES = 4 * MIB
BF16 = jnp.bfloat16
F32 = jnp.float32


def _rms(x, g):
    ms = jnp.mean(x * x, axis=-1, keepdims=True)
    return x * lax.rsqrt(ms + EPS) * g


def _dot(a, b):
    return jnp.dot(a, b, preferred_element_type=F32)


def _pre_kernel(x_ref, gmix_ref, win_ref, gq_ref, gkv_ref, wqa_ref, wqb_ref,
                wkv_ref, rope_ref, u_ref, q_ref, k_ref, v_ref, *, q_scale):
    x = x_ref[0]
    tile = x.shape[0]
    h = _rms(x, gmix_ref[...]).astype(BF16)
    z = _dot(h, win_ref[...])
    u_ref[0] = z[:, :POOL_WIDTH]
    o = POOL_WIDTH
    qn = _rms(z[:, o:o + Q_LORA_RANK], gq_ref[...]).astype(BF16)
    o += Q_LORA_RANK
    kvn = _rms(z[:, o:o + KV_LORA_RANK], gkv_ref[...]).astype(BF16)
    o += KV_LORA_RANK
    cos_e = rope_ref[:, 0:LANES]
    sin_e = rope_ref[:, LANES:2 * LANES]
    cos_o = pltpu.roll(cos_e, LANES // 2, axis=1)
    sin_o = pltpu.roll(sin_e, LANES // 2, axis=1)
    lane = lax.broadcasted_iota(jnp.int32, (tile, LANES), 1)
    low_half = lane < LANES // 2
    kblk = z[:, o:o + LANES]
    kr_o = kblk * cos_o + pltpu.roll(kblk, LANES - QK_ROPE_DIM, axis=1) * sin_o
    kr_o = jnp.where(lane < QK_NOPE_DIM + QK_ROPE_DIM, kr_o, 0.0)
    kr_e = pltpu.roll(kr_o, LANES // 2, axis=1)
    one_e = jnp.where(lane == V_HEAD_DIM, 1.0, 0.0)
    one_o = jnp.where(lane == 0, 1.0, 0.0)
    qa = _dot(qn, wqa_ref[...])
    qb = _dot(qn, wqb_ref[...])
    kv = _dot(kvn, wkv_ref[...])
    for hd in range(N_HEADS):
        sl = slice(hd * HEAD_PAD, (hd + 1) * HEAD_PAD)
        even = hd % 2 == 0
        cos, sin = (cos_e, sin_e) if even else (cos_o, sin_o)
        q_ref[0, hd] = ((qa[:, sl] * cos + qb[:, sl] * sin) * q_scale).astype(BF16)
        blk = kv[:, sl]
        if even:
            k_ref[0, hd] = jnp.where(low_half, kr_e, blk).astype(BF16)
            v_ref[0, hd] = jnp.where(low_half, blk, one_e).astype(BF16)
        else:
            k_ref[0, hd] = jnp.where(low_half, blk, kr_o).astype(BF16)
            v_ref[0, hd] = jnp.where(low_half, one_o, blk).astype(BF16)


def _pre_call(x, p, rope, tile):
    B, S, _ = x.shape
    grid = (B, S // tile)
    const2 = lambda b, i: (0, 0)
    q_scale = math.log2(math.e) / math.sqrt(QK_HEAD_DIM)
    head_shape = jax.ShapeDtypeStruct((B, N_HEADS, S, HEAD_PAD), BF16)
    head_spec = pl.BlockSpec((1, N_HEADS, tile, HEAD_PAD), lambda b, i: (b, 0, i, 0))
    return pl.pallas_call(
        functools.partial(_pre_kernel, q_scale=q_scale),
        grid=grid,
        in_specs=[
            pl.BlockSpec((1, tile, D_MODEL), lambda b, i: (b, i, 0)),
            pl.BlockSpec((1, D_MODEL), const2),
            pl.BlockSpec(p['w_in'].shape, const2),
            pl.BlockSpec((1, Q_LORA_RANK), const2),
            pl.BlockSpec((1, KV_LORA_RANK), const2),
            pl.BlockSpec(p['w_qa'].shape, const2),
            pl.BlockSpec(p['w_qb'].shape, const2),
            pl.BlockSpec(p['w_kv'].shape, const2),
            pl.BlockSpec((tile, 2 * LANES), lambda b, i: (i, 0)),
        ],
        out_specs=[
            pl.BlockSpec((1, tile, POOL_WIDTH), lambda b, i: (b, i, 0)),
            head_spec, head_spec, head_spec,
        ],
        out_shape=[
            jax.ShapeDtypeStruct((B, S, POOL_WIDTH), F32),
            head_shape, head_shape, head_shape,
        ],
        compiler_params=pltpu.CompilerParams(
            dimension_semantics=("parallel", "parallel"),
            vmem_limit_bytes=VMEM_LIMIT),
        name="pre_proj",
    )(x, p['g_mix'], p['w_in'], p['g_q'], p['g_kv'], p['w_qa'], p['w_qb'],
      p['w_kv'], rope)


def _attn_kernel(q_ref, k_ref, v_ref, o_ref, *, q_sub, k_chunk):
    tq = q_ref.shape[2]
    S = k_ref.shape[2]
    n_chunks = S // k_chunk
    lane = lax.broadcasted_iota(jnp.int32, (q_sub, LANES), 1)

    def attend(hd, row0):
        q = q_ref[0, hd, pl.ds(row0, q_sub), :]
        m = acc = None
        for c in range(n_chunks):
            k = k_ref[0, hd, pl.ds(c * k_chunk, k_chunk), :]
            v = v_ref[0, hd, pl.ds(c * k_chunk, k_chunk), :]
            s = lax.dot_general(q, k, (((1,), (1,)), ((), ())),
                                preferred_element_type=F32)
            m_c = jnp.max(s, axis=-1, keepdims=True)
            m_new = m_c if c == 0 else jnp.maximum(m, m_c)
            pr = jnp.exp2(s - m_new).astype(BF16)
            pv = _dot(pr, v)
            acc = pv if c == 0 else acc * jnp.exp2(m - m_new) + pv
            m = m_new
        return acc

    def body(r, carry):
        row0 = pl.multiple_of(r * q_sub, q_sub)
        a0 = attend(0, row0)
        a1 = attend(1, row0)
        l0 = a0[:, V_HEAD_DIM:V_HEAD_DIM + 1]
        l1 = a1[:, 0:1]
        out = jnp.where(lane < V_HEAD_DIM, a0 / l0, a1 / l1)
        o_ref[0, pl.ds(row0, q_sub), :] = out.astype(o_ref.dtype)
        return carry

    lax.fori_loop(0, tq // q_sub, body, 0, unroll=2)


def _attn_call(q, k, v, tq, q_sub, k_chunk):
    B, H, S, _ = q.shape
    grid = (B, H // 2, S // tq)
    return pl.pallas_call(
        functools.partial(_attn_kernel, q_sub=q_sub, k_chunk=k_chunk),
        grid=grid,
        in_specs=[
            pl.BlockSpec((1, 2, tq, HEAD_PAD), lambda b, j, i: (b, j, i, 0)),
            pl.BlockSpec((1, 2, S, HEAD_PAD), lambda b, j, i: (b, j, 0, 0)),
            pl.BlockSpec((1, 2, S, HEAD_PAD), lambda b, j, i: (b, j, 0, 0)),
        ],
        out_specs=pl.BlockSpec((1, tq, LANES), lambda b, j, i: (b, i, j)),
        out_shape=jax.ShapeDtypeStruct((B, S, (H // 2) * LANES), BF16),
        compiler_params=pltpu.CompilerParams(
            dimension_semantics=("parallel", "parallel", "parallel"),
            vmem_limit_bytes=VMEM_LIMIT),
        name="attn",
    )(q, k, v)


def _mix_kernel(up_ref, um_ref, un_ref, ya_ref, x_ref, wpool_ref, pscale_ref,
                wout_ref, gffn_ref, x1_ref, h2_ref, ext_ref, s2_ref, s4_ref, s8_ref,
                *, seq_len):
    tile = um_ref.shape[1]
    i = pl.program_id(1)
    n_tiles = pl.num_programs(1)
    H8 = HALO_F32
    G = POOL_GROUP_DIM
    ext_ref[0:H8, :] = jnp.where(i > 0, up_ref[0], 0.0)
    ext_ref[H8:H8 + tile, :] = um_ref[0]
    ext_ref[H8 + tile:2 * H8 + tile, :] = jnp.where(i < n_tiles - 1, un_ref[0], 0.0)
    ext_ref[2 * H8 + tile:, :] = jnp.zeros((2 * H8, POOL_WIDTH), F32)
    n2, n4, n8 = tile + 3 * H8, tile + 2 * H8, tile + H8
    s2_ref[...] = ext_ref[0:n2, G:] + ext_ref[1:n2 + 1, G:]
    s4_ref[...] = s2_ref[0:n4, :] + s2_ref[2:n4 + 2, :]
    s8_ref[...] = s4_ref[0:n8, G:] + s4_ref[4:n8 + 4, G:]
    window_sums = (
        ext_ref[H8 - 1:H8 - 1 + tile, 0:G] + ext_ref[H8:H8 + tile, 0:G],
        s4_ref[H8 - 2:H8 - 2 + tile, 0:G],
        s8_ref[H8 - 4:H8 - 4 + tile, 0:G],
        s8_ref[0:tile, G:] + s8_ref[H8:H8 + tile, G:],
    )
    pos = i * tile + lax.broadcasted_iota(jnp.int32, (tile, 1), 0)
    parts = []
    for g, w in enumerate(POOL_WINDOWS):
        cs = slice(g * G, (g + 1) * G)
        half = w // 2
        lo = jnp.maximum(pos - half, 0)
        hi = jnp.minimum(pos + half, seq_len)
        cnt = (hi - lo).astype(F32)
        pooled = window_sums[g] / cnt - ext_ref[H8:H8 + tile, cs]
        pm = _dot(pooled.astype(BF16), wpool_ref[g]) * pscale_ref[:, cs]
        parts.append(pm.astype(BF16))
    mixed = jnp.concatenate(parts + [ya_ref[0]], axis=-1)
    x1 = x_ref[0] + _dot(mixed, wout_ref[...])
    x1_ref[0] = x1
    h2_ref[0] = _rms(x1, gffn_ref[...]).astype(BF16)


def _mix_call(u, ya, x, p, tile):
    B, S, _ = x.shape
    grid = (B, S // tile)
    tb = tile // HALO_F32
    nb = S // HALO_F32
    const2 = lambda b, i: (0, 0)
    return pl.pallas_call(
        functools.partial(_mix_kernel, seq_len=S),
        grid=grid,
        in_specs=[
            pl.BlockSpec((1, HALO_F32, POOL_WIDTH),
                         lambda b, i: (b, jnp.maximum(i * tb - 1, 0), 0)),
            pl.BlockSpec((1, tile, POOL_WIDTH), lambda b, i: (b, i, 0)),
            pl.BlockSpec((1, HALO_F32, POOL_WIDTH),
                         lambda b, i: (b, jnp.minimum((i + 1) * tb, nb - 1), 0)),
            pl.BlockSpec((1, tile, POOL_WIDTH), lambda b, i: (b, i, 0)),
            pl.BlockSpec((1, tile, D_MODEL), lambda b, i: (b, i, 0)),
            pl.BlockSpec(p['w_pool'].shape, lambda b, i: (0, 0, 0)),
            pl.BlockSpec((1, POOL_WIDTH), const2),
            pl.BlockSpec((D_MODEL, D_MODEL), const2),
            pl.BlockSpec((1, D_MODEL), const2),
        ],
        out_specs=[
            pl.BlockSpec((1, tile, D_MODEL), lambda b, i: (b, i, 0)),
            pl.BlockSpec((1, tile, D_MODEL), lambda b, i: (b, i, 0)),
        ],
        out_shape=[
            jax.ShapeDtypeStruct((B, S, D_MODEL), F32),
            jax.ShapeDtypeStruct((B, S, D_MODEL), BF16),
        ],
        scratch_shapes=[
            pltpu.VMEM((tile + 4 * HALO_F32, POOL_WIDTH), F32),
            pltpu.VMEM((tile + 3 * HALO_F32, 3 * POOL_GROUP_DIM), F32),
            pltpu.VMEM((tile + 2 * HALO_F32, 3 * POOL_GROUP_DIM), F32),
            pltpu.VMEM((tile + HALO_F32, 2 * POOL_GROUP_DIM), F32),
        ],
        compiler_params=pltpu.CompilerParams(
            dimension_semantics=("parallel", "parallel"),
            vmem_limit_bytes=VMEM_LIMIT),
        name="mix_out",
    )(u, u, u, ya, x, p['w_pool'], p['pool_scale'], p['w_out'], p['g_ffn'])


def _ffn_kernel(hp_ref, hm_ref, hn_ref, x1_ref, wup_ref, cw_ref, cb_ref,
                wdn_ref, gfin_ref, y_ref, hext_ref, ua_ref, ub_ref, acc_ref):
    tile = hm_ref.shape[1]
    i = pl.program_id(1)
    n_tiles = pl.num_programs(1)
    H16 = HALO_BF16
    hext_ref[0:H16, :] = jnp.where(i > 0, hp_ref[0], jnp.zeros_like(hp_ref[0]))
    hext_ref[H16:H16 + tile, :] = hm_ref[0]
    hext_ref[H16 + tile:, :] = jnp.where(i < n_tiles - 1, hn_ref[0],
                                         jnp.zeros_like(hn_ref[0]))
    acc_ref[...] = jnp.zeros_like(acc_ref)

    def up_proj(c, u_ref):
        u_ref[...] = _dot(hext_ref[...], wup_ref[c])

    def conv_down(c, u_ref):
        cw = cw_ref[c]
        conv = (cb_ref[c]
                + u_ref[H16 - 1:H16 - 1 + tile, :] * cw[0:1, :]
                + u_ref[H16:H16 + tile, :] * cw[1:2, :]
                + u_ref[H16 + 1:H16 + 1 + tile, :] * cw[2:3, :])
        gate = conv[:, :F_CHUNK]
        val = conv[:, F_CHUNK:]
        act = (gate * jax.nn.sigmoid(gate) * val).astype(BF16)
        acc_ref[...] += _dot(act, wdn_ref[c])

    up_proj(0, ua_ref)

    def chunk_pair(k, carry):
        c = 2 * k
        up_proj(c + 1, ub_ref)
        conv_down(c, ua_ref)
        up_proj(c + 2, ua_ref)
        conv_down(c + 1, ub_ref)
        return carry

    assert N_F_CHUNKS % 2 == 1
    lax.fori_loop(0, N_F_CHUNKS // 2, chunk_pair, 0)
    conv_down(N_F_CHUNKS - 1, ua_ref)
    y_ref[0] = _rms(x1_ref[0] + acc_ref[...], gfin_ref[...])


def _ffn_call(h2, x1, p, tile):
    B, S, _ = x1.shape
    grid = (B, S // tile)
    tb = tile // HALO_BF16
    nb = S // HALO_BF16
    const2 = lambda b, i: (0, 0)
    const3 = lambda b, i: (0, 0, 0)
    resident = dict(pipeline_mode=pl.Buffered(1))
    return pl.pallas_call(
        _ffn_kernel,
        grid=grid,
        in_specs=[
            pl.BlockSpec((1, HALO_BF16, D_MODEL),
                         lambda b, i: (b, jnp.maximum(i * tb - 1, 0), 0)),
            pl.BlockSpec((1, tile, D_MODEL), lambda b, i: (b, i, 0)),
            pl.BlockSpec((1, HALO_BF16, D_MODEL),
                         lambda b, i: (b, jnp.minimum((i + 1) * tb, nb - 1), 0)),
            pl.BlockSpec((1, tile, D_MODEL), lambda b, i: (b, i, 0)),
            pl.BlockSpec(p['w_up'].shape, const3, **resident),
            pl.BlockSpec(p['conv_w'].shape, const3, **resident),
            pl.BlockSpec(p['conv_b'].shape, const3, **resident),
            pl.BlockSpec(p['w_down'].shape, const3, **resident),
            pl.BlockSpec((1, D_MODEL), const2),
        ],
        out_specs=pl.BlockSpec((1, tile, D_MODEL), lambda b, i: (b, i, 0)),
        out_shape=jax.ShapeDtypeStruct((B, S, D_MODEL), F32),
        scratch_shapes=[
            pltpu.VMEM((tile + 2 * HALO_BF16, D_MODEL), BF16),
            pltpu.VMEM((tile + 2 * HALO_BF16, 2 * F_CHUNK), F32),
            pltpu.VMEM((tile + 2 * HALO_BF16, 2 * F_CHUNK), F32),
            pltpu.VMEM((tile, D_MODEL), F32),
        ],
        compiler_params=pltpu.CompilerParams(
            dimension_semantics=("parallel", "parallel"),
            vmem_limit_bytes=FFN_VMEM_LIMIT),
        name="conv_ffn",
    )(h2, h2, h2, x1, p['w_up'], p['conv_w'], p['conv_b'], p['w_down'], p['g_fin'])


def _prep_params(norm_mix_g, w_in, q_norm_g, w_uq, kv_norm_g, w_ukv, w_pool,
                 pool_scale, w_out, norm_ffn_g, w_up, conv_w, conv_b, w_down,
                 final_norm_g):
    R = HALF_ROPE
    nope = QK_NOPE_DIM
    zeros = lambda r, c: jnp.zeros((r, c), F32)

    o = POOL_WIDTH + Q_LORA_RANK + KV_LORA_RANK
    kx1 = w_in[:, o:o + R]
    kx2 = w_in[:, o + R:o + 2 * R]
    w_in_p = jnp.concatenate(
        [w_in[:, :o], zeros(D_MODEL, nope), kx1, kx2, kx2, kx1], axis=1)

    qa, qb, wkv = [], [], []
    zq = zeros(Q_LORA_RANK, LANES - nope - 2 * R)
    for h in range(N_HEADS):
        c0 = h * QK_HEAD_DIM
        qn_ = w_uq[:, c0:c0 + nope]
        qx1 = w_uq[:, c0 + nope:c0 + nope + R]
        qx2 = w_uq[:, c0 + nope + R:c0 + nope + 2 * R]
        k0 = h * (nope + V_HEAD_DIM)
        kcols = w_ukv[:, k0:k0 + nope]
        vcols = w_ukv[:, k0 + nope:k0 + nope + V_HEAD_DIM]
        if h % 2 == 0:
            qa += [qx1, qx2, zq, qn_]
            qb += [qx2, qx1, zq, zeros(Q_LORA_RANK, nope)]
            wkv += [vcols, kcols]
        else:
            qa += [qn_, qx1, qx2, zq]
            qb += [zeros(Q_LORA_RANK, nope), qx2, qx1, zq]
            wkv += [kcols, vcols]

    nc = N_F_CHUNKS
    w_up_b = w_up.astype(BF16)
    gate_w = w_up_b[:, :D_FF].reshape(D_MODEL, nc, F_CHUNK)
    val_w = w_up_b[:, D_FF:].reshape(D_MODEL, nc, F_CHUNK)
    w_up_c = jnp.concatenate([gate_w, val_w], axis=2).transpose(1, 0, 2)
    cw_c = jnp.concatenate([conv_w[:, :D_FF].reshape(3, nc, F_CHUNK),
                            conv_w[:, D_FF:].reshape(3, nc, F_CHUNK)],
                           axis=2).transpose(1, 0, 2)
    cb_c = jnp.concatenate([conv_b[:D_FF].reshape(nc, 1, F_CHUNK),
                            conv_b[D_FF:].reshape(nc, 1, F_CHUNK)], axis=2)
    return dict(
        g_mix=norm_mix_g[None, :], w_in=w_in_p.astype(BF16),
        g_q=q_norm_g[None, :], g_kv=kv_norm_g[None, :],
        w_qa=jnp.concatenate(qa, axis=1).astype(BF16),
        w_qb=jnp.concatenate(qb, axis=1).astype(BF16),
        w_kv=jnp.concatenate(wkv, axis=1).astype(BF16),
        w_pool=w_pool.astype(BF16), pool_scale=pool_scale[None, :],
        w_out=w_out.astype(BF16), g_ffn=norm_ffn_g[None, :],
        w_up=w_up_c, conv_w=cw_c, conv_b=cb_c,
        w_down=w_down.reshape(nc, F_CHUNK, D_MODEL).astype(BF16),
        g_fin=final_norm_g[None, :],
    )


def _rope_tables(seq):
    inv_freq = ROPE_THETA ** (-jnp.arange(0, QK_ROPE_DIM, 2, dtype=F32) / QK_ROPE_DIM)
    ang = jnp.arange(seq, dtype=F32)[:, None] * inv_freq[None, :]
    c, s = jnp.cos(ang), jnp.sin(ang)
    pad = ((0, 0), (0, LANES - QK_ROPE_DIM))
    cos_e = jnp.pad(jnp.concatenate([c, c], axis=1), pad, constant_values=1.0)
    sin_e = jnp.pad(jnp.concatenate([-s, s], axis=1), pad)
    return jnp.concatenate([cos_e, sin_e], axis=1)


def _tile_plan(seq):
    row_tile = min(ROW_TILE, seq)
    tq = min(ATTN_Q_TILE, seq)
    k_chunk = min(ATTN_MAX_KEY_CHUNK, seq // 2)
    q_sub = min(tq, SCORE_TILE_BYTES // 4 // k_chunk)
    assert seq % row_tile == 0 and seq % tq == 0 and seq % k_chunk == 0
    assert tq % (2 * q_sub) == 0
    return row_tile, tq, q_sub, k_chunk


def _encoder(x, p):
    S = x.shape[1]
    row_tile, tq, q_sub, k_chunk = _tile_plan(S)
    u, q, k, v = _pre_call(x, p, _rope_tables(S), tile=row_tile)
    ya = _attn_call(q, k, v, tq=tq, q_sub=q_sub, k_chunk=k_chunk)
    x1, h2 = _mix_call(u, ya, x, p, tile=row_tile)
    return _ffn_call(h2, x1, p, tile=row_tile)


def kernel(x_prompt, x_sample, norm_mix_g, w_in, q_norm_g, w_uq, kv_norm_g, w_ukv, w_pool, pool_scale, w_out, norm_ffn_g, w_up, conv_w, conv_b, w_down, final_norm_g):
    assert norm_mix_g.shape[0] == 1
    p = _prep_params(norm_mix_g[0], w_in[0], q_norm_g[0], w_uq[0], kv_norm_g[0],
                     w_ukv[0], w_pool[0], pool_scale[0], w_out[0], norm_ffn_g[0],
                     w_up[0], conv_w[0], conv_b[0], w_down[0], final_norm_g)
    return (_encoder(x_prompt, p), _encoder(x_sample, p))
```

```python
import functools
import math

import jax
import jax.numpy as jnp
from jax import lax
from jax.experimental import pallas as pl
from jax.experimental.pallas import tpu as pltpu

D_MODEL = 1024
POOL_WINDOWS = (2, 4, 8, 16)
POOL_WIDTH = 512
POOL_GROUP_DIM = 128
N_HEADS = 8
QK_NOPE_DIM = 64
QK_ROPE_DIM = 32
QK_HEAD_DIM = 96
V_HEAD_DIM = 64
Q_LORA_RANK = 256
KV_LORA_RANK = 128
ROPE_THETA = 10000.0
D_FF = 2816
EPS = 1e-6

LANES = 128
HEAD_PAD = LANES
HALF_ROPE = QK_ROPE_DIM // 2
F_CHUNK = 256
assert D_FF % F_CHUNK == 0
N_F_CHUNKS = D_FF // F_CHUNK
HALO_F32 = 8
HALO_BF16 = 16
MIB = 1024 * 1024
VMEM_LIMIT = 48 * MIB
FFN_VMEM_LIMIT = 56 * MIB
ROW_TILE = 1024
ATTN_Q_TILE = 2048
ATTN_MAX_KEY_CHUNK = 2048
SCORE_TILE_BYTES = 4 * MIB
BF16 = jnp.bfloat16
F32 = jnp.float32


def _rms(x, g):
    ms = jnp.mean(x * x, axis=-1, keepdims=True)
    return x * lax.rsqrt(ms + EPS) * g


def _dot(a, b):
    return jnp.dot(a, b, preferred_element_type=F32)


def _pre_kernel(x_ref, gmix_ref, win_ref, gq_ref, gkv_ref, wqa_ref, wqb_ref,
                wkv_ref, rope_ref, u_ref, q_ref, k_ref, v_ref, *, q_scale):
    x = x_ref[0]
    tile = x.shape[0]
    h = _rms(x, gmix_ref[...]).astype(BF16)
    z = _dot(h, win_ref[...])
    u_ref[0] = z[:, :POOL_WIDTH]
    o = POOL_WIDTH
    qn = _rms(z[:, o:o + Q_LORA_RANK], gq_ref[...]).astype(BF16)
    o += Q_LORA_RANK
    kvn = _rms(z[:, o:o + KV_LORA_RANK], gkv_ref[...]).astype(BF16)
    o += KV_LORA_RANK
    cos_e = rope_ref[:, 0:LANES]
    sin_e = rope_ref[:, LANES:2 * LANES]
    cos_o = pltpu.roll(cos_e, LANES // 2, axis=1)
    sin_o = pltpu.roll(sin_e, LANES // 2, axis=1)
    lane = lax.broadcasted_iota(jnp.int32, (tile, LANES), 1)
    low_half = lane < LANES // 2
    kblk = z[:, o:o + LANES]
    kr_o = kblk * cos_o + pltpu.roll(kblk, LANES - QK_ROPE_DIM, axis=1) * sin_o
    kr_o = jnp.where(lane < QK_NOPE_DIM + QK_ROPE_DIM, kr_o, 0.0)
    kr_e = pltpu.roll(kr_o, LANES // 2, axis=1)
    one_e = jnp.where(lane == V_HEAD_DIM, 1.0, 0.0)
    one_o = jnp.where(lane == 0, 1.0, 0.0)
    qa = _dot(qn, wqa_ref[...])
    qb = _dot(qn, wqb_ref[...])
    kv = _dot(kvn, wkv_ref[...])
    for hd in range(N_HEADS):
        sl = slice(hd * HEAD_PAD, (hd + 1) * HEAD_PAD)
        even = hd % 2 == 0
        cos, sin = (cos_e, sin_e) if even else (cos_o, sin_o)
        q_ref[0, hd] = ((qa[:, sl] * cos + qb[:, sl] * sin) * q_scale).astype(BF16)
        blk = kv[:, sl]
        if even:
            k_ref[0, hd] = jnp.where(low_half, kr_e, blk).astype(BF16)
            v_ref[0, hd] = jnp.where(low_half, blk, one_e).astype(BF16)
        else:
            k_ref[0, hd] = jnp.where(low_half, blk, kr_o).astype(BF16)
            v_ref[0, hd] = jnp.where(low_half, one_o, blk).astype(BF16)


def _pre_call(x, p, rope, tile):
    B, S, _ = x.shape
    grid = (B, S // tile)
    const2 = lambda b, i: (0, 0)
    q_scale = math.log2(math.e) / math.sqrt(QK_HEAD_DIM)
    head_shape = jax.ShapeDtypeStruct((B, N_HEADS, S, HEAD_PAD), BF16)
    head_spec = pl.BlockSpec((1, N_HEADS, tile, HEAD_PAD), lambda b, i: (b, 0, i, 0))
    return pl.pallas_call(
        functools.partial(_pre_kernel, q_scale=q_scale),
        grid=grid,
        in_specs=[
            pl.BlockSpec((1, tile, D_MODEL), lambda b, i: (b, i, 0)),
            pl.BlockSpec((1, D_MODEL), const2),
            pl.BlockSpec(p['w_in'].shape, const2),
            pl.BlockSpec((1, Q_LORA_RANK), const2),
            pl.BlockSpec((1, KV_LORA_RANK), const2),
            pl.BlockSpec(p['w_qa'].shape, const2),
            pl.BlockSpec(p['w_qb'].shape, const2),
            pl.BlockSpec(p['w_kv'].shape, const2),
            pl.BlockSpec((tile, 2 * LANES), lambda b, i: (i, 0)),
        ],
        out_specs=[
            pl.BlockSpec((1, tile, POOL_WIDTH), lambda b, i: (b, i, 0)),
            head_spec, head_spec, head_spec,
        ],
        out_shape=[
            jax.ShapeDtypeStruct((B, S, POOL_WIDTH), F32),
            head_shape, head_shape, head_shape,
        ],
        compiler_params=pltpu.CompilerParams(
            dimension_semantics=("parallel", "parallel"),
            vmem_limit_bytes=VMEM_LIMIT),
        name="pre_proj",
    )(x, p['g_mix'], p['w_in'], p['g_q'], p['g_kv'], p['w_qa'], p['w_qb'],
      p['w_kv'], rope)


def _attn_kernel(q_ref, k_ref, v_ref, o_ref, *, q_sub, k_chunk):
    tq = q_ref.shape[2]
    S = k_ref.shape[2]
    n_chunks = S // k_chunk
    lane = lax.broadcasted_iota(jnp.int32, (q_sub, LANES), 1)

    def attend(hd, row0):
        q = q_ref[0, hd, pl.ds(row0, q_sub), :]
        m = acc = None
        for c in range(n_chunks):
            k = k_ref[0, hd, pl.ds(c * k_chunk, k_chunk), :]
            v = v_ref[0, hd, pl.ds(c * k_chunk, k_chunk), :]
            s = lax.dot_general(q, k, (((1,), (1,)), ((), ())),
                                preferred_element_type=F32)
            m_c = jnp.max(s, axis=-1, keepdims=True)
            m_new = m_c if c == 0 else jnp.maximum(m, m_c)
            pr = jnp.exp2(s - m_new).astype(BF16)
            pv = _dot(pr, v)
            acc = pv if c == 0 else acc * jnp.exp2(m - m_new) + pv
            m = m_new
        return acc

    def body(r, carry):
        row0 = pl.multiple_of(r * q_sub, q_sub)
        a0 = attend(0, row0)
        a1 = attend(1, row0)
        l0 = a0[:, V_HEAD_DIM:V_HEAD_DIM + 1]
        l1 = a1[:, 0:1]
        out = jnp.where(lane < V_HEAD_DIM, a0 / l0, a1 / l1)
        o_ref[0, pl.ds(row0, q_sub), :] = out.astype(o_ref.dtype)
        return carry

    lax.fori_loop(0, tq // q_sub, body, 0, unroll=2)


def _attn_call(q, k, v, tq, q_sub, k_chunk):
    B, H, S, _ = q.shape
    grid = (B, H // 2, S // tq)
    return pl.pallas_call(
        functools.partial(_attn_kernel, q_sub=q_sub, k_chunk=k_chunk),
        grid=grid,
        in_specs=[
            pl.BlockSpec((1, 2, tq, HEAD_PAD), lambda b, j, i: (b, j, i, 0)),
            pl.BlockSpec((1, 2, S, HEAD_PAD), lambda b, j, i: (b, j, 0, 0)),
            pl.BlockSpec((1, 2, S, HEAD_PAD), lambda b, j, i: (b, j, 0, 0)),
        ],
        out_specs=pl.BlockSpec((1, tq, LANES), lambda b, j, i: (b, i, j)),
        out_shape=jax.ShapeDtypeStruct((B, S, (H // 2) * LANES), BF16),
        compiler_params=pltpu.CompilerParams(
            dimension_semantics=("parallel", "parallel", "parallel"),
            vmem_limit_bytes=VMEM_LIMIT),
        name="attn",
    )(q, k, v)


def _mix_kernel(up_ref, um_ref, un_ref, ya_ref, x_ref, wpool_ref, pscale_ref,
                wout_ref, gffn_ref, x1_ref, h2_ref, ext_ref, s2_ref, s4_ref, s8_ref,
                *, seq_len):
    tile = um_ref.shape[1]
    i = pl.program_id(1)
    n_tiles = pl.num_programs(1)
    H8 = HALO_F32
    G = POOL_GROUP_DIM
    ext_ref[0:H8, :] = jnp.where(i > 0, up_ref[0], 0.0)
    ext_ref[H8:H8 + tile, :] = um_ref[0]
    ext_ref[H8 + tile:2 * H8 + tile, :] = jnp.where(i < n_tiles - 1, un_ref[0], 0.0)
    ext_ref[2 * H8 + tile:, :] = jnp.zeros((2 * H8, POOL_WIDTH), F32)
    n2, n4, n8 = tile + 3 * H8, tile + 2 * H8, tile + H8
    s2_ref[...] = ext_ref[0:n2, G:] + ext_ref[1:n2 + 1, G:]
    s4_ref[...] = s2_ref[0:n4, :] + s2_ref[2:n4 + 2, :]
    s8_ref[...] = s4_ref[0:n8, G:] + s4_ref[4:n8 + 4, G:]
    window_sums = (
        ext_ref[H8 - 1:H8 - 1 + tile, 0:G] + ext_ref[H8:H8 + tile, 0:G],
        s4_ref[H8 - 2:H8 - 2 + tile, 0:G],
        s8_ref[H8 - 4:H8 - 4 + tile, 0:G],
        s8_ref[0:tile, G:] + s8_ref[H8:H8 + tile, G:],
    )
    pos = i * tile + lax.broadcasted_iota(jnp.int32, (tile, 1), 0)
    parts = []
    for g, w in enumerate(POOL_WINDOWS):
        cs = slice(g * G, (g + 1) * G)
        half = w // 2
        lo = jnp.maximum(pos - half, 0)
        hi = jnp.minimum(pos + half, seq_len)
        cnt = (hi - lo).astype(F32)
        pooled = window_sums[g] / cnt - ext_ref[H8:H8 + tile, cs]
        pm = _dot(pooled.astype(BF16), wpool_ref[g]) * pscale_ref[:, cs]
        parts.append(pm.astype(BF16))
    mixed = jnp.concatenate(parts + [ya_ref[0]], axis=-1)
    x1 = x_ref[0] + _dot(mixed, wout_ref[...])
    x1_ref[0] = x1
    h2_ref[0] = _rms(x1, gffn_ref[...]).astype(BF16)


def _mix_call(u, ya, x, p, tile):
    B, S, _ = x.shape
    grid = (B, S // tile)
    tb = tile // HALO_F32
    nb = S // HALO_F32
    const2 = lambda b, i: (0, 0)
    return pl.pallas_call(
        functools.partial(_mix_kernel, seq_len=S),
        grid=grid,
        in_specs=[
            pl.BlockSpec((1, HALO_F32, POOL_WIDTH),
                         lambda b, i: (b, jnp.maximum(i * tb - 1, 0), 0)),
            pl.BlockSpec((1, tile, POOL_WIDTH), lambda b, i: (b, i, 0)),
            pl.BlockSpec((1, HALO_F32, POOL_WIDTH),
                         lambda b, i: (b, jnp.minimum((i + 1) * tb, nb - 1), 0)),
            pl.BlockSpec((1, tile, POOL_WIDTH), lambda b, i: (b, i, 0)),
            pl.BlockSpec((1, tile, D_MODEL), lambda b, i: (b, i, 0)),
            pl.BlockSpec(p['w_pool'].shape, lambda b, i: (0, 0, 0)),
            pl.BlockSpec((1, POOL_WIDTH), const2),
            pl.BlockSpec((D_MODEL, D_MODEL), const2),
            pl.BlockSpec((1, D_MODEL), const2),
        ],
        out_specs=[
            pl.BlockSpec((1, tile, D_MODEL), lambda b, i: (b, i, 0)),
            pl.BlockSpec((1, tile, D_MODEL), lambda b, i: (b, i, 0)),
        ],
        out_shape=[
            jax.ShapeDtypeStruct((B, S, D_MODEL), F32),
            jax.ShapeDtypeStruct((B, S, D_MODEL), BF16),
        ],
        scratch_shapes=[
            pltpu.VMEM((tile + 4 * HALO_F32, POOL_WIDTH), F32),
            pltpu.VMEM((tile + 3 * HALO_F32, 3 * POOL_GROUP_DIM), F32),
            pltpu.VMEM((tile + 2 * HALO_F32, 3 * POOL_GROUP_DIM), F32),
            pltpu.VMEM((tile + HALO_F32, 2 * POOL_GROUP_DIM), F32),
        ],
        compiler_params=pltpu.CompilerParams(
            dimension_semantics=("parallel", "parallel"),
            vmem_limit_bytes=VMEM_LIMIT),
        name="mix_out",
    )(u, u, u, ya, x, p['w_pool'], p['pool_scale'], p['w_out'], p['g_ffn'])


def _ffn_kernel(hp_ref, hm_ref, hn_ref, x1_ref, wup_ref, cw_ref, cb_ref,
                wdn_ref, gfin_ref, y_ref, hext_ref, ua_ref, ub_ref, acc_ref):
    tile = hm_ref.shape[1]
    i = pl.program_id(1)
    n_tiles = pl.num_programs(1)
    H16 = HALO_BF16
    hext_ref[0:H16, :] = jnp.where(i > 0, hp_ref[0], jnp.zeros_like(hp_ref[0]))
    hext_ref[H16:H16 + tile, :] = hm_ref[0]
    hext_ref[H16 + tile:, :] = jnp.where(i < n_tiles - 1, hn_ref[0],
                                         jnp.zeros_like(hn_ref[0]))
    acc_ref[...] = jnp.zeros_like(acc_ref)

    def up_proj(c, u_ref):
        u_ref[...] = _dot(hext_ref[...], wup_ref[c])

    def conv_down(c, u_ref):
        cw = cw_ref[c]
        conv = (cb_ref[c]
                + u_ref[H16 - 1:H16 - 1 + tile, :] * cw[0:1, :]
                + u_ref[H16:H16 + tile, :] * cw[1:2, :]
                + u_ref[H16 + 1:H16 + 1 + tile, :] * cw[2:3, :])
        gate = conv[:, :F_CHUNK].astype(BF16)
        val = conv[:, F_CHUNK:].astype(BF16)
        act = gate * jax.nn.sigmoid(gate) * val
        acc_ref[...] += _dot(act, wdn_ref[c])

    up_proj(0, ua_ref)

    def chunk_pair(k, carry):
        c = 2 * k
        up_proj(c + 1, ub_ref)
        conv_down(c, ua_ref)
        up_proj(c + 2, ua_ref)
        conv_down(c + 1, ub_ref)
        return carry

    assert N_F_CHUNKS % 2 == 1
    lax.fori_loop(0, N_F_CHUNKS // 2, chunk_pair, 0)
    conv_down(N_F_CHUNKS - 1, ua_ref)
    y_ref[0] = _rms(x1_ref[0] + acc_ref[...], gfin_ref[...])


def _ffn_call(h2, x1, p, tile):
    B, S, _ = x1.shape
    grid = (B, S // tile)
    tb = tile // HALO_BF16
    nb = S // HALO_BF16
    const2 = lambda b, i: (0, 0)
    const3 = lambda b, i: (0, 0, 0)
    resident = dict(pipeline_mode=pl.Buffered(1))
    return pl.pallas_call(
        _ffn_kernel,
        grid=grid,
        in_specs=[
            pl.BlockSpec((1, HALO_BF16, D_MODEL),
                         lambda b, i: (b, jnp.maximum(i * tb - 1, 0), 0)),
            pl.BlockSpec((1, tile, D_MODEL), lambda b, i: (b, i, 0)),
            pl.BlockSpec((1, HALO_BF16, D_MODEL),
                         lambda b, i: (b, jnp.minimum((i + 1) * tb, nb - 1), 0)),
            pl.BlockSpec((1, tile, D_MODEL), lambda b, i: (b, i, 0)),
            pl.BlockSpec(p['w_up'].shape, const3, **resident),
            pl.BlockSpec(p['conv_w'].shape, const3, **resident),
            pl.BlockSpec(p['conv_b'].shape, const3, **resident),
            pl.BlockSpec(p['w_down'].shape, const3, **resident),
            pl.BlockSpec((1, D_MODEL), const2),
        ],
        out_specs=pl.BlockSpec((1, tile, D_MODEL), lambda b, i: (b, i, 0)),
        out_shape=jax.ShapeDtypeStruct((B, S, D_MODEL), F32),
        scratch_shapes=[
            pltpu.VMEM((tile + 2 * HALO_BF16, D_MODEL), BF16),
            pltpu.VMEM((tile + 2 * HALO_BF16, 2 * F_CHUNK), F32),
            pltpu.VMEM((tile + 2 * HALO_BF16, 2 * F_CHUNK), F32),
            pltpu.VMEM((tile, D_MODEL), F32),
        ],
        compiler_params=pltpu.CompilerParams(
            dimension_semantics=("parallel", "parallel"),
            vmem_limit_bytes=FFN_VMEM_LIMIT),
        name="conv_ffn",
    )(h2, h2, h2, x1, p['w_up'], p['conv_w'], p['conv_b'], p['w_down'], p['g_fin'])


def _prep_params(norm_mix_g, w_in, q_norm_g, w_uq, kv_norm_g, w_ukv, w_pool,
                 pool_scale, w_out, norm_ffn_g, w_up, conv_w, conv_b, w_down,
                 final_norm_g):
    R = HALF_ROPE
    nope = QK_NOPE_DIM
    zeros = lambda r, c: jnp.zeros((r, c), F32)

    o = POOL_WIDTH + Q_LORA_RANK + KV_LORA_RANK
    kx1 = w_in[:, o:o + R]
    kx2 = w_in[:, o + R:o + 2 * R]
    w_in_p = jnp.concatenate(
        [w_in[:, :o], zeros(D_MODEL, nope), kx1, kx2, kx2, kx1], axis=1)

    qa, qb, wkv = [], [], []
    zq = zeros(Q_LORA_RANK, LANES - nope - 2 * R)
    for h in range(N_HEADS):
        c0 = h * QK_HEAD_DIM
        qn_ = w_uq[:, c0:c0 + nope]
        qx1 = w_uq[:, c0 + nope:c0 + nope + R]
        qx2 = w_uq[:, c0 + nope + R:c0 + nope + 2 * R]
        k0 = h * (nope + V_HEAD_DIM)
        kcols = w_ukv[:, k0:k0 + nope]
        vcols = w_ukv[:, k0 + nope:k0 + nope + V_HEAD_DIM]
        if h % 2 == 0:
            qa += [qx1, qx2, zq, qn_]
            qb += [qx2, qx1, zq, zeros(Q_LORA_RANK, nope)]
            wkv += [vcols, kcols]
        else:
            qa += [qn_, qx1, qx2, zq]
            qb += [zeros(Q_LORA_RANK, nope), qx2, qx1, zq]
            wkv += [kcols, vcols]

    nc = N_F_CHUNKS
    w_up_b = w_up.astype(BF16)
    gate_w = w_up_b[:, :D_FF].reshape(D_MODEL, nc, F_CHUNK)
    val_w = w_up_b[:, D_FF:].reshape(D_MODEL, nc, F_CHUNK)
    w_up_c = jnp.concatenate([gate_w, val_w], axis=2).transpose(1, 0, 2)
    cw_c = jnp.concatenate([conv_w[:, :D_FF].reshape(3, nc, F_CHUNK),
                            conv_w[:, D_FF:].reshape(3, nc, F_CHUNK)],
                           axis=2).transpose(1, 0, 2)
    cb_c = jnp.concatenate([conv_b[:D_FF].reshape(nc, 1, F_CHUNK),
                            conv_b[D_FF:].reshape(nc, 1, F_CHUNK)], axis=2)
    return dict(
        g_mix=norm_mix_g[None, :], w_in=w_in_p.astype(BF16),
        g_q=q_norm_g[None, :], g_kv=kv_norm_g[None, :],
        w_qa=jnp.concatenate(qa, axis=1).astype(BF16),
        w_qb=jnp.concatenate(qb, axis=1).astype(BF16),
        w_kv=jnp.concatenate(wkv, axis=1).astype(BF16),
        w_pool=w_pool.astype(BF16), pool_scale=pool_scale[None, :],
        w_out=w_out.astype(BF16), g_ffn=norm_ffn_g[None, :],
        w_up=w_up_c, conv_w=cw_c, conv_b=cb_c,
        w_down=w_down.reshape(nc, F_CHUNK, D_MODEL).astype(BF16),
        g_fin=final_norm_g[None, :],
    )


def _rope_tables(seq):
    inv_freq = ROPE_THETA ** (-jnp.arange(0, QK_ROPE_DIM, 2, dtype=F32) / QK_ROPE_DIM)
    ang = jnp.arange(seq, dtype=F32)[:, None] * inv_freq[None, :]
    c, s = jnp.cos(ang), jnp.sin(ang)
    pad = ((0, 0), (0, LANES - QK_ROPE_DIM))
    cos_e = jnp.pad(jnp.concatenate([c, c], axis=1), pad, constant_values=1.0)
    sin_e = jnp.pad(jnp.concatenate([-s, s], axis=1), pad)
    return jnp.concatenate([cos_e, sin_e], axis=1)


def _tile_plan(seq):
    row_tile = min(ROW_TILE, seq)
    tq = min(ATTN_Q_TILE, seq)
    k_chunk = min(ATTN_MAX_KEY_CHUNK, seq // 2)
    q_sub = min(tq, SCORE_TILE_BYTES // 4 // k_chunk)
    assert seq % row_tile == 0 and seq % tq == 0 and seq % k_chunk == 0
    assert tq % (2 * q_sub) == 0
    return row_tile, tq, q_sub, k_chunk


def _encoder(x, p):
    S = x.shape[1]
    row_tile, tq, q_sub, k_chunk = _tile_plan(S)
    u, q, k, v = _pre_call(x, p, _rope_tables(S), tile=row_tile)
    ya = _attn_call(q, k, v, tq=tq, q_sub=q_sub, k_chunk=k_chunk)
    x1, h2 = _mix_call(u, ya, x, p, tile=row_tile)
    return _ffn_call(h2, x1, p, tile=row_tile)


def kernel(x_prompt, x_sample, norm_mix_g, w_in, q_norm_g, w_uq, kv_norm_g, w_ukv, w_pool, pool_scale, w_out, norm_ffn_g, w_up, conv_w, conv_b, w_down, final_norm_g):
    assert norm_mix_g.shape[0] == 1
    p = _prep_params(norm_mix_g[0], w_in[0], q_norm_g[0], w_uq[0], kv_norm_g[0],
                     w_ukv[0], w_pool[0], pool_scale[0], w_out[0], norm_ffn_g[0],
                     w_up[0], conv_w[0], conv_b[0], w_down[0], final_norm_g)
    return (_encoder(x_prompt, p), _encoder(x_sample, p))
```

```python
import functools
import math

import jax
import jax.numpy as jnp
from jax import lax
from jax.experimental import pallas as pl
from jax.experimental.pallas import tpu as pltpu

D_MODEL = 1024
POOL_WINDOWS = (2, 4, 8, 16)
POOL_WIDTH = 512
POOL_GROUP_DIM = 128
N_HEADS = 8
QK_NOPE_DIM = 64
QK_ROPE_DIM = 32
QK_HEAD_DIM = 96
V_HEAD_DIM = 64
Q_LORA_RANK = 256
KV_LORA_RANK = 128
ROPE_THETA = 10000.0
D_FF = 2816
EPS = 1e-6

LANES = 128
HEAD_PAD = LANES
HALF_ROPE = QK_ROPE_DIM // 2
F_CHUNK = 256
assert D_FF % F_CHUNK == 0
N_F_CHUNKS = D_FF // F_CHUNK
HALO_F32 = 8
HALO_BF16 = 16
MIB = 1024 * 1024
VMEM_LIMIT = 48 * MIB
FFN_VMEM_LIMIT = 56 * MIB
ROW_TILE = 1024
ATTN_Q_TILE = 2048
ATTN_MAX_KEY_CHUNK = 2048
SCORE_TILE_BYTES = 4 * MIB
BF16 = jnp.bfloat16
F32 = jnp.float32


def _rms(x, g):
    ms = jnp.mean(x * x, axis=-1, keepdims=True)
    return x * lax.rsqrt(ms + EPS) * g


def _dot(a, b):
    return jnp.dot(a, b, preferred_element_type=F32)


def _pre_kernel(x_ref, gmix_ref, win_ref, gq_ref, gkv_ref, wqa_ref, wqb_ref,
                wkv_ref, rope_ref, u_ref, q_ref, k_ref, v_ref, *, q_scale):
    x = x_ref[0]
    tile = x.shape[0]
    h = _rms(x, gmix_ref[...]).astype(BF16)
    z = _dot(h, win_ref[...])
    u_ref[0] = z[:, :POOL_WIDTH]
    o = POOL_WIDTH
    qn = _rms(z[:, o:o + Q_LORA_RANK], gq_ref[...]).astype(BF16)
    o += Q_LORA_RANK
    kvn = _rms(z[:, o:o + KV_LORA_RANK], gkv_ref[...]).astype(BF16)
    o += KV_LORA_RANK
    cos_e = rope_ref[:, 0:LANES]
    sin_e = rope_ref[:, LANES:2 * LANES]
    cos_o = pltpu.roll(cos_e, LANES // 2, axis=1)
    sin_o = pltpu.roll(sin_e, LANES // 2, axis=1)
    lane = lax.broadcasted_iota(jnp.int32, (tile, LANES), 1)
    low_half = lane < LANES // 2
    kblk = z[:, o:o + LANES]
    kr_o = kblk * cos_o + pltpu.roll(kblk, LANES - QK_ROPE_DIM, axis=1) * sin_o
    kr_o = jnp.where(lane < QK_NOPE_DIM + QK_ROPE_DIM, kr_o, 0.0)
    kr_e = pltpu.roll(kr_o, LANES // 2, axis=1)
    one_e = jnp.where(lane == V_HEAD_DIM, 1.0, 0.0)
    one_o = jnp.where(lane == 0, 1.0, 0.0)
    qa = _dot(qn, wqa_ref[...])
    qb = _dot(qn, wqb_ref[...])
    kv = _dot(kvn, wkv_ref[...])
    for hd in range(N_HEADS):
        sl = slice(hd * HEAD_PAD, (hd + 1) * HEAD_PAD)
        even = hd % 2 == 0
        cos, sin = (cos_e, sin_e) if even else (cos_o, sin_o)
        q_ref[0, hd] = ((qa[:, sl] * cos + qb[:, sl] * sin) * q_scale).astype(BF16)
        blk = kv[:, sl]
        if even:
            k_ref[0, hd] = jnp.where(low_half, kr_e, blk).astype(BF16)
            v_ref[0, hd] = jnp.where(low_half, blk, one_e).astype(BF16)
        else:
            k_ref[0, hd] = jnp.where(low_half, blk, kr_o).astype(BF16)
            v_ref[0, hd] = jnp.where(low_half, one_o, blk).astype(BF16)


def _pre_call(x, p, rope, tile):
    B, S, _ = x.shape
    grid = (B, S // tile)
    const2 = lambda b, i: (0, 0)
    q_scale = math.log2(math.e) / math.sqrt(QK_HEAD_DIM)
    head_shape = jax.ShapeDtypeStruct((B, N_HEADS, S, HEAD_PAD), BF16)
    head_spec = pl.BlockSpec((1, N_HEADS, tile, HEAD_PAD), lambda b, i: (b, 0, i, 0))
    return pl.pallas_call(
        functools.partial(_pre_kernel, q_scale=q_scale),
        grid=grid,
        in_specs=[
            pl.BlockSpec((1, tile, D_MODEL), lambda b, i: (b, i, 0)),
            pl.BlockSpec((1, D_MODEL), const2),
            pl.BlockSpec(p['w_in'].shape, const2),
            pl.BlockSpec((1, Q_LORA_RANK), const2),
            pl.BlockSpec((1, KV_LORA_RANK), const2),
            pl.BlockSpec(p['w_qa'].shape, const2),
            pl.BlockSpec(p['w_qb'].shape, const2),
            pl.BlockSpec(p['w_kv'].shape, const2),
            pl.BlockSpec((tile, 2 * LANES), lambda b, i: (i, 0)),
        ],
        out_specs=[
            pl.BlockSpec((1, tile, POOL_WIDTH), lambda b, i: (b, i, 0)),
            head_spec, head_spec, head_spec,
        ],
        out_shape=[
            jax.ShapeDtypeStruct((B, S, POOL_WIDTH), F32),
            head_shape, head_shape, head_shape,
        ],
        compiler_params=pltpu.CompilerParams(
            dimension_semantics=("parallel", "parallel"),
            vmem_limit_bytes=VMEM_LIMIT),
        name="pre_proj",
    )(x, p['g_mix'], p['w_in'], p['g_q'], p['g_kv'], p['w_qa'], p['w_qb'],
      p['w_kv'], rope)


def _attn_kernel(q_ref, k_ref, v_ref, o_ref, *, q_sub, k_chunk):
    tq = q_ref.shape[2]
    S = k_ref.shape[2]
    n_chunks = S // k_chunk
    lane = lax.broadcasted_iota(jnp.int32, (q_sub, LANES), 1)

    def attend(hd, row0):
        q = q_ref[0, hd, pl.ds(row0, q_sub), :]
        m = acc = None
        for c in range(n_chunks):
            k = k_ref[0, hd, pl.ds(c * k_chunk, k_chunk), :]
            v = v_ref[0, hd, pl.ds(c * k_chunk, k_chunk), :]
            s = lax.dot_general(q, k, (((1,), (1,)), ((), ())),
                                preferred_element_type=F32)
            m_c = jnp.max(s, axis=-1, keepdims=True)
            m_new = m_c if c == 0 else jnp.maximum(m, m_c)
            pr = jnp.exp2((s - m_new).astype(BF16))
            pv = _dot(pr, v)
            acc = pv if c == 0 else acc * jnp.exp2(m - m_new) + pv
            m = m_new
        return acc

    def body(r, carry):
        row0 = pl.multiple_of(r * q_sub, q_sub)
        a0 = attend(0, row0)
        a1 = attend(1, row0)
        l0 = a0[:, V_HEAD_DIM:V_HEAD_DIM + 1]
        l1 = a1[:, 0:1]
        out = jnp.where(lane < V_HEAD_DIM, a0 / l0, a1 / l1)
        o_ref[0, pl.ds(row0, q_sub), :] = out.astype(o_ref.dtype)
        return carry

    lax.fori_loop(0, tq // q_sub, body, 0, unroll=2)


def _attn_call(q, k, v, tq, q_sub, k_chunk):
    B, H, S, _ = q.shape
    grid = (B, H // 2, S // tq)
    return pl.pallas_call(
        functools.partial(_attn_kernel, q_sub=q_sub, k_chunk=k_chunk),
        grid=grid,
        in_specs=[
            pl.BlockSpec((1, 2, tq, HEAD_PAD), lambda b, j, i: (b, j, i, 0)),
            pl.BlockSpec((1, 2, S, HEAD_PAD), lambda b, j, i: (b, j, 0, 0)),
            pl.BlockSpec((1, 2, S, HEAD_PAD), lambda b, j, i: (b, j, 0, 0)),
        ],
        out_specs=pl.BlockSpec((1, tq, LANES), lambda b, j, i: (b, i, j)),
        out_shape=jax.ShapeDtypeStruct((B, S, (H // 2) * LANES), BF16),
        compiler_params=pltpu.CompilerParams(
            dimension_semantics=("parallel", "parallel", "parallel"),
            vmem_limit_bytes=VMEM_LIMIT),
        name="attn",
    )(q, k, v)


def _mix_kernel(up_ref, um_ref, un_ref, ya_ref, x_ref, wpool_ref, pscale_ref,
                wout_ref, gffn_ref, x1_ref, h2_ref, ext_ref, s2_ref, s4_ref, s8_ref,
                *, seq_len):
    tile = um_ref.shape[1]
    i = pl.program_id(1)
    n_tiles = pl.num_programs(1)
    H8 = HALO_F32
    G = POOL_GROUP_DIM
    ext_ref[0:H8, :] = jnp.where(i > 0, up_ref[0], 0.0)
    ext_ref[H8:H8 + tile, :] = um_ref[0]
    ext_ref[H8 + tile:2 * H8 + tile, :] = jnp.where(i < n_tiles - 1, un_ref[0], 0.0)
    ext_ref[2 * H8 + tile:, :] = jnp.zeros((2 * H8, POOL_WIDTH), F32)
    n2, n4, n8 = tile + 3 * H8, tile + 2 * H8, tile + H8
    s2_ref[...] = ext_ref[0:n2, G:] + ext_ref[1:n2 + 1, G:]
    s4_ref[...] = s2_ref[0:n4, :] + s2_ref[2:n4 + 2, :]
    s8_ref[...] = s4_ref[0:n8, G:] + s4_ref[4:n8 + 4, G:]
    window_sums = (
        ext_ref[H8 - 1:H8 - 1 + tile, 0:G] + ext_ref[H8:H8 + tile, 0:G],
        s4_ref[H8 - 2:H8 - 2 + tile, 0:G],
        s8_ref[H8 - 4:H8 - 4 + tile, 0:G],
        s8_ref[0:tile, G:] + s8_ref[H8:H8 + tile, G:],
    )
    pos = i * tile + lax.broadcasted_iota(jnp.int32, (tile, 1), 0)
    parts = []
    for g, w in enumerate(POOL_WINDOWS):
        cs = slice(g * G, (g + 1) * G)
        half = w // 2
        lo = jnp.maximum(pos - half, 0)
        hi = jnp.minimum(pos + half, seq_len)
        cnt = (hi - lo).astype(F32)
        pooled = window_sums[g] / cnt - ext_ref[H8:H8 + tile, cs]
        pm = _dot(pooled.astype(BF16), wpool_ref[g]) * pscale_ref[:, cs]
        parts.append(pm.astype(BF16))
    mixed = jnp.concatenate(parts + [ya_ref[0]], axis=-1)
    x1 = x_ref[0] + _dot(mixed, wout_ref[...])
    x1_ref[0] = x1
    h2_ref[0] = _rms(x1, gffn_ref[...]).astype(BF16)


def _mix_call(u, ya, x, p, tile):
    B, S, _ = x.shape
    grid = (B, S // tile)
    tb = tile // HALO_F32
    nb = S // HALO_F32
    const2 = lambda b, i: (0, 0)
    return pl.pallas_call(
        functools.partial(_mix_kernel, seq_len=S),
        grid=grid,
        in_specs=[
            pl.BlockSpec((1, HALO_F32, POOL_WIDTH),
                         lambda b, i: (b, jnp.maximum(i * tb - 1, 0), 0)),
            pl.BlockSpec((1, tile, POOL_WIDTH), lambda b, i: (b, i, 0)),
            pl.BlockSpec((1, HALO_F32, POOL_WIDTH),
                         lambda b, i: (b, jnp.minimum((i + 1) * tb, nb - 1), 0)),
            pl.BlockSpec((1, tile, POOL_WIDTH), lambda b, i: (b, i, 0)),
            pl.BlockSpec((1, tile, D_MODEL), lambda b, i: (b, i, 0)),
            pl.BlockSpec(p['w_pool'].shape, lambda b, i: (0, 0, 0)),
            pl.BlockSpec((1, POOL_WIDTH), const2),
            pl.BlockSpec((D_MODEL, D_MODEL), const2),
            pl.BlockSpec((1, D_MODEL), const2),
        ],
        out_specs=[
            pl.BlockSpec((1, tile, D_MODEL), lambda b, i: (b, i, 0)),
            pl.BlockSpec((1, tile, D_MODEL), lambda b, i: (b, i, 0)),
        ],
        out_shape=[
            jax.ShapeDtypeStruct((B, S, D_MODEL), F32),
            jax.ShapeDtypeStruct((B, S, D_MODEL), BF16),
        ],
        scratch_shapes=[
            pltpu.VMEM((tile + 4 * HALO_F32, POOL_WIDTH), F32),
            pltpu.VMEM((tile + 3 * HALO_F32, 3 * POOL_GROUP_DIM), F32),
            pltpu.VMEM((tile + 2 * HALO_F32, 3 * POOL_GROUP_DIM), F32),
            pltpu.VMEM((tile + HALO_F32, 2 * POOL_GROUP_DIM), F32),
        ],
        compiler_params=pltpu.CompilerParams(
            dimension_semantics=("parallel", "parallel"),
            vmem_limit_bytes=VMEM_LIMIT),
        name="mix_out",
    )(u, u, u, ya, x, p['w_pool'], p['pool_scale'], p['w_out'], p['g_ffn'])


def _ffn_kernel(hp_ref, hm_ref, hn_ref, x1_ref, wup_ref, cw_ref, cb_ref,
                wdn_ref, gfin_ref, y_ref, hext_ref, ua_ref, ub_ref, acc_ref):
    tile = hm_ref.shape[1]
    i = pl.program_id(1)
    n_tiles = pl.num_programs(1)
    H16 = HALO_BF16
    hext_ref[0:H16, :] = jnp.where(i > 0, hp_ref[0], jnp.zeros_like(hp_ref[0]))
    hext_ref[H16:H16 + tile, :] = hm_ref[0]
    hext_ref[H16 + tile:, :] = jnp.where(i < n_tiles - 1, hn_ref[0],
                                         jnp.zeros_like(hn_ref[0]))
    acc_ref[...] = jnp.zeros_like(acc_ref)

    def up_proj(c, u_ref):
        u_ref[...] = _dot(hext_ref[...], wup_ref[c])

    def conv_down(c, u_ref):
        cw = cw_ref[c]
        conv = (cb_ref[c]
                + u_ref[H16 - 1:H16 - 1 + tile, :] * cw[0:1, :]
                + u_ref[H16:H16 + tile, :] * cw[1:2, :]
                + u_ref[H16 + 1:H16 + 1 + tile, :] * cw[2:3, :])
        gate = conv[:, :F_CHUNK].astype(BF16)
        val = conv[:, F_CHUNK:].astype(BF16)
        act = gate * jax.nn.sigmoid(gate) * val
        acc_ref[...] += _dot(act, wdn_ref[c])

    up_proj(0, ua_ref)

    def chunk_pair(k, carry):
        c = 2 * k
        up_proj(c + 1, ub_ref)
        conv_down(c, ua_ref)
        up_proj(c + 2, ua_ref)
        conv_down(c + 1, ub_ref)
        return carry

    assert N_F_CHUNKS % 2 == 1
    lax.fori_loop(0, N_F_CHUNKS // 2, chunk_pair, 0)
    conv_down(N_F_CHUNKS - 1, ua_ref)
    y_ref[0] = _rms(x1_ref[0] + acc_ref[...], gfin_ref[...])


def _ffn_call(h2, x1, p, tile):
    B, S, _ = x1.shape
    grid = (B, S // tile)
    tb = tile // HALO_BF16
    nb = S // HALO_BF16
    const2 = lambda b, i: (0, 0)
    const3 = lambda b, i: (0, 0, 0)
    resident = dict(pipeline_mode=pl.Buffered(1))
    return pl.pallas_call(
        _ffn_kernel,
        grid=grid,
        in_specs=[
            pl.BlockSpec((1, HALO_BF16, D_MODEL),
                         lambda b, i: (b, jnp.maximum(i * tb - 1, 0), 0)),
            pl.BlockSpec((1, tile, D_MODEL), lambda b, i: (b, i, 0)),
            pl.BlockSpec((1, HALO_BF16, D_MODEL),
                         lambda b, i: (b, jnp.minimum((i + 1) * tb, nb - 1), 0)),
            pl.BlockSpec((1, tile, D_MODEL), lambda b, i: (b, i, 0)),
            pl.BlockSpec(p['w_up'].shape, const3, **resident),
            pl.BlockSpec(p['conv_w'].shape, const3, **resident),
            pl.BlockSpec(p['conv_b'].shape, const3, **resident),
            pl.BlockSpec(p['w_down'].shape, const3, **resident),
            pl.BlockSpec((1, D_MODEL), const2),
        ],
        out_specs=pl.BlockSpec((1, tile, D_MODEL), lambda b, i: (b, i, 0)),
        out_shape=jax.ShapeDtypeStruct((B, S, D_MODEL), F32),
        scratch_shapes=[
            pltpu.VMEM((tile + 2 * HALO_BF16, D_MODEL), BF16),
            pltpu.VMEM((tile + 2 * HALO_BF16, 2 * F_CHUNK), F32),
            pltpu.VMEM((tile + 2 * HALO_BF16, 2 * F_CHUNK), F32),
            pltpu.VMEM((tile, D_MODEL), F32),
        ],
        compiler_params=pltpu.CompilerParams(
            dimension_semantics=("parallel", "parallel"),
            vmem_limit_bytes=FFN_VMEM_LIMIT),
        name="conv_ffn",
    )(h2, h2, h2, x1, p['w_up'], p['conv_w'], p['conv_b'], p['w_down'], p['g_fin'])


def _prep_params(norm_mix_g, w_in, q_norm_g, w_uq, kv_norm_g, w_ukv, w_pool,
                 pool_scale, w_out, norm_ffn_g, w_up, conv_w, conv_b, w_down,
                 final_norm_g):
    R = HALF_ROPE
    nope = QK_NOPE_DIM
    zeros = lambda r, c: jnp.zeros((r, c), F32)

    o = POOL_WIDTH + Q_LORA_RANK + KV_LORA_RANK
    kx1 = w_in[:, o:o + R]
    kx2 = w_in[:, o + R:o + 2 * R]
    w_in_p = jnp.concatenate(
        [w_in[:, :o], zeros(D_MODEL, nope), kx1, kx2, kx2, kx1], axis=1)

    qa, qb, wkv = [], [], []
    zq = zeros(Q_LORA_RANK, LANES - nope - 2 * R)
    for h in range(N_HEADS):
        c0 = h * QK_HEAD_DIM
        qn_ = w_uq[:, c0:c0 + nope]
        qx1 = w_uq[:, c0 + nope:c0 + nope + R]
        qx2 = w_uq[:, c0 + nope + R:c0 + nope + 2 * R]
        k0 = h * (nope + V_HEAD_DIM)
        kcols = w_ukv[:, k0:k0 + nope]
        vcols = w_ukv[:, k0 + nope:k0 + nope + V_HEAD_DIM]
        if h % 2 == 0:
            qa += [qx1, qx2, zq, qn_]
            qb += [qx2, qx1, zq, zeros(Q_LORA_RANK, nope)]
            wkv += [vcols, kcols]
        else:
            qa += [qn_, qx1, qx2, zq]
            qb += [zeros(Q_LORA_RANK, nope), qx2, qx1, zq]
            wkv += [kcols, vcols]

    nc = N_F_CHUNKS
    w_up_b = w_up.astype(BF16)
    gate_w = w_up_b[:, :D_FF].reshape(D_MODEL, nc, F_CHUNK)
    val_w = w_up_b[:, D_FF:].reshape(D_MODEL, nc, F_CHUNK)
    w_up_c = jnp.concatenate([gate_w, val_w], axis=2).transpose(1, 0, 2)
    cw_c = jnp.concatenate([conv_w[:, :D_FF].reshape(3, nc, F_CHUNK),
                            conv_w[:, D_FF:].reshape(3, nc, F_CHUNK)],
                           axis=2).transpose(1, 0, 2)
    cb_c = jnp.concatenate([conv_b[:D_FF].reshape(nc, 1, F_CHUNK),
                            conv_b[D_FF:].reshape(nc, 1, F_CHUNK)], axis=2)
    return dict(
        g_mix=norm_mix_g[None, :], w_in=w_in_p.astype(BF16),
        g_q=q_norm_g[None, :], g_kv=kv_norm_g[None, :],
        w_qa=jnp.concatenate(qa, axis=1).astype(BF16),
        w_qb=jnp.concatenate(qb, axis=1).astype(BF16),
        w_kv=jnp.concatenate(wkv, axis=1).astype(BF16),
        w_pool=w_pool.astype(BF16), pool_scale=pool_scale[None, :],
        w_out=w_out.astype(BF16), g_ffn=norm_ffn_g[None, :],
        w_up=w_up_c, conv_w=cw_c, conv_b=cb_c,
        w_down=w_down.reshape(nc, F_CHUNK, D_MODEL).astype(BF16),
        g_fin=final_norm_g[None, :],
    )


def _rope_tables(seq):
    inv_freq = ROPE_THETA ** (-jnp.arange(0, QK_ROPE_DIM, 2, dtype=F32) / QK_ROPE_DIM)
    ang = jnp.arange(seq, dtype=F32)[:, None] * inv_freq[None, :]
    c, s = jnp.cos(ang), jnp.sin(ang)
    pad = ((0, 0), (0, LANES - QK_ROPE_DIM))
    cos_e = jnp.pad(jnp.concatenate([c, c], axis=1), pad, constant_values=1.0)
    sin_e = jnp.pad(jnp.concatenate([-s, s], axis=1), pad)
    return jnp.concatenate([cos_e, sin_e], axis=1)


def _tile_plan(seq):
    row_tile = min(ROW_TILE, seq)
    tq = min(ATTN_Q_TILE, seq)
    k_chunk = min(ATTN_MAX_KEY_CHUNK, seq // 2)
    q_sub = min(tq, SCORE_TILE_BYTES // 4 // k_chunk)
    assert seq % row_tile == 0 and seq % tq == 0 and seq % k_chunk == 0
    assert tq % (2 * q_sub) == 0
    return row_tile, tq, q_sub, k_chunk


def _encoder(x, p):
    S = x.shape[1]
    row_tile, tq, q_sub, k_chunk = _tile_plan(S)
    u, q, k, v = _pre_call(x, p, _rope_tables(S), tile=row_tile)
    ya = _attn_call(q, k, v, tq=tq, q_sub=q_sub, k_chunk=k_chunk)
    x1, h2 = _mix_call(u, ya, x, p, tile=row_tile)
    return _ffn_call(h2, x1, p, tile=row_tile)


def kernel(x_prompt, x_sample, norm_mix_g, w_in, q_norm_g, w_uq, kv_norm_g, w_ukv, w_pool, pool_scale, w_out, norm_ffn_g, w_up, conv_w, conv_b, w_down, final_norm_g):
    assert norm_mix_g.shape[0] == 1
    p = _prep_params(norm_mix_g[0], w_in[0], q_norm_g[0], w_uq[0], kv_norm_g[0],
                     w_ukv[0], w_pool[0], pool_scale[0], w_out[0], norm_ffn_g[0],
                     w_up[0], conv_w[0], conv_b[0], w_down[0], final_norm_g)
    return (_encoder(x_prompt, p), _encoder(x_sample, p))
```
